```python
import math
import jax
import jax.numpy as jnp
from jax import lax
import numpy as np

D_MODEL = 1024
BATCH = 8
SEQ = 4096
DEPTH = 4

GLA_HEADS = 4
GLA_DK = D_MODEL // (4 * GLA_HEADS)
GLA_DV = D_MODEL // (2 * GLA_HEADS)
GLA_GATE_RANK = 16
GLA_GATE_TEMP = 16.0
GLA_CHUNK = 64
MOBA_HEADS = 8
MOBA_DH = D_MODEL // (2 * MOBA_HEADS)
MOBA_BLOCK = 256
MOBA_TOPK = 3
MOBA_QCHUNK = 32
CONV_CH = D_MODEL // 2
CONV_K = 3
POOL_CH = D_MODEL // 2
POOL_WINDOWS = (2, 4, 8, 16)
POOL_GROUP = POOL_CH // len(POOL_WINDOWS)
D_FF = 4 * D_MODEL
EPS = 1e-6

GLA_QK = GLA_HEADS * GLA_DK
GLA_V = GLA_HEADS * GLA_DV
MOBA_W = MOBA_HEADS * MOBA_DH
EVEN_SPLITS = (GLA_QK, GLA_QK, GLA_V, GLA_V, GLA_GATE_RANK, MOBA_W, MOBA_W, MOBA_W)
EVEN_IN = sum(EVEN_SPLITS)
ODD_SPLITS = (CONV_CH, CONV_CH, CONV_CH, POOL_CH)
ODD_IN = sum(ODD_SPLITS)
N_EVEN = (DEPTH + 1) // 2
N_ODD = DEPTH // 2

kernel_name = "hybrid_gla_moba_conv_pool_trunk"


def _split(a, sizes):
    return jnp.split(a, np.cumsum(sizes)[:-1].tolist(), axis=-1)


def rmsnorm(x, g):
    xf = x.astype(jnp.float32)
    y = xf * lax.rsqrt(jnp.mean(xf * xf, axis=-1, keepdims=True) + EPS)
    return (y * g.astype(jnp.float32)).astype(x.dtype)


def gla_chunked(q, k, v, log_a):
    B, S, H, dk = q.shape
    dv = v.shape[-1]
    C = GLA_CHUNK
    N = S // C

    def blk(t, d):
        return t.astype(jnp.float32).reshape(B, N, C, H, d).transpose(0, 3, 1, 2, 4)

    q = blk(q, dk) * (dk ** -0.5)
    k = blk(k, dk)
    v = blk(v, dv)
    g = blk(log_a, dk)
    b = jnp.cumsum(g, axis=3)
    q_ = q * jnp.exp(b)
    k_ = k * jnp.exp(-b)
    causal = jnp.tril(jnp.ones((C, C), dtype=bool))
    att = jnp.where(causal, jnp.einsum('bhnid,bhnjd->bhnij', q_, k_), 0.0)
    o_intra = jnp.einsum('bhnij,bhnjv->bhniv', att, v)
    b_last = b[:, :, :, -1:, :]
    k_end = k * jnp.exp(b_last - b)
    d_state = jnp.einsum('bhnjd,bhnjv->bhndv', k_end, v)
    decay = jnp.exp(b_last[:, :, :, 0, :])

    def step(state, inp):
        dcy, ds = inp
        return dcy[..., None] * state + ds, state

    init = jnp.zeros((B, H, dk, dv), jnp.float32)
    _, s_in = lax.scan(step, init, (jnp.moveaxis(decay, 2, 0), jnp.moveaxis(d_state, 2, 0)))
    s_in = jnp.moveaxis(s_in, 0, 2)
    o = o_intra + jnp.einsum('bhnid,bhndv->bhniv', q_, s_in)
    return o.transpose(0, 2, 3, 1, 4).reshape(B, S, H, dv)


def alibi_slopes(n_heads):
    return jnp.exp2(-8.0 * jnp.arange(1, n_heads + 1, dtype=jnp.float32) / n_heads)


def moba_attention(q, k, v):
    B, S, H, dh = q.shape
    BS = MOBA_BLOCK
    QC = MOBA_QCHUNK
    NB = -(-S // BS)
    pad = NB * BS - S
    q = q.transpose(0, 2, 1, 3)
    kp = jnp.pad(k.transpose(0, 2, 1, 3), ((0, 0), (0, 0), (0, pad), (0, 0)))
    vp = jnp.pad(v.transpose(0, 2, 1, 3), ((0, 0), (0, 0), (0, pad), (0, 0)))
    kb = kp.reshape(B, H, NB, BS, dh)
    vb = vp.reshape(B, H, NB, BS, dh)
    kmean = jnp.mean(kb.astype(jnp.float32), axis=3)
    slopes = alibi_slopes(H)
    scale = dh ** -0.5
    k_sel = min(MOBA_TOPK, NB - 1)
    gather = jax.vmap(jax.vmap(lambda blocks, ix: blocks[ix]))

    def chunk(c):
        t0 = c * QC
        blk = t0 // BS
        t = t0 + jnp.arange(QC)
        qc = lax.dynamic_slice_in_dim(q, t0, QC, axis=2)
        ko = lax.dynamic_index_in_dim(kb, blk, axis=2, keepdims=False)
        vo = lax.dynamic_index_in_dim(vb, blk, axis=2, keepdims=False)
        pos_o = blk * BS + jnp.arange(BS)
        dist_o = (t[:, None] - pos_o[None, :]).astype(jnp.float32)
        s_o = jnp.einsum('bhqd,bhpd->bhqp', qc, ko, preferred_element_type=jnp.float32) * scale
        s_o = s_o - slopes[None, :, None, None] * dist_o
        s_o = jnp.where(pos_o[None, :] <= t[:, None], s_o, -jnp.inf)
        if k_sel == 0:
            p = jax.nn.softmax(s_o, axis=-1)
            return jnp.einsum('bhqp,bhpd->bhqd', p.astype(vo.dtype), vo)
        gate = jnp.einsum('bhqd,bhnd->bhqn', qc.astype(jnp.float32), kmean)
        gate = jnp.where(jnp.arange(NB) < blk, gate, -jnp.inf)
        top_v, top_i = lax.top_k(gate, k_sel)
        valid = jnp.isfinite(top_v)
        kg = gather(kb, top_i)
        vg = gather(vb, top_i)
        pos_g = top_i[..., None] * BS + jnp.arange(BS)
        dist_g = (t[:, None, None] - pos_g).astype(jnp.float32)
        s_g = jnp.einsum('bhqd,bhqkpd->bhqkp', qc, kg, preferred_element_type=jnp.float32) * scale
        s_g = s_g - slopes[None, :, None, None, None] * dist_g
        s_g = jnp.where(valid[..., None], s_g, -jnp.inf)
        logits = jnp.concatenate([s_g.reshape(B, H, QC, k_sel * BS), s_o], axis=-1)
        p = jax.nn.softmax(logits, axis=-1)
        p_g = p[..., :k_sel * BS].reshape(B, H, QC, k_sel, BS).astype(vg.dtype)
        p_o = p[..., k_sel * BS:].astype(vo.dtype)
        return (jnp.einsum('bhqkp,bhqkpd->bhqd', p_g, vg)
                + jnp.einsum('bhqp,bhpd->bhqd', p_o, vo))

    outs = lax.map(chunk, jnp.arange(S // QC))
    return outs.transpose(1, 0, 3, 2, 4).reshape(B, S, H * dh)


def mix_even(h, w_in, w_gate2, b_gate, gla_norm, q_norm, k_norm, w_o):
    B, S, _ = h.shape
    proj = h @ w_in
    qa, ka, va, ra, glr, qb, kb, vb = _split(proj, EVEN_SPLITS)
    log_a = jax.nn.log_sigmoid((glr @ w_gate2 + b_gate).astype(jnp.float32)) / GLA_GATE_TEMP
    oa = gla_chunked(qa.reshape(B, S, GLA_HEADS, GLA_DK), ka.reshape(B, S, GLA_HEADS, GLA_DK),
                     va.reshape(B, S, GLA_HEADS, GLA_DV), log_a.reshape(B, S, GLA_HEADS, GLA_DK))
    oa = rmsnorm(oa, gla_norm).reshape(B, S, GLA_V) * jax.nn.silu(ra.astype(jnp.float32))
    qb = rmsnorm(qb.reshape(B, S, MOBA_HEADS, MOBA_DH), q_norm)
    kb = rmsnorm(kb.reshape(B, S, MOBA_HEADS, MOBA_DH), k_norm)
    ob = moba_attention(qb, kb, vb.reshape(B, S, MOBA_HEADS, MOBA_DH))
    o = jnp.concatenate([oa.astype(h.dtype), ob.astype(h.dtype)], axis=-1)
    return o @ w_o


def causal_dwconv(z, w):
    C = z.shape[-1]
    return lax.conv_general_dilated(
        z, w.reshape(CONV_K, 1, C).astype(z.dtype), window_strides=(1,),
        padding=[(CONV_K - 1, 0)], dimension_numbers=('NWC', 'WIO', 'NWC'),
        feature_group_count=C)


def multiscale_pool(u, pool_w, pool_scale):
    B, S, _ = u.shape
    uf = u.astype(jnp.float32)
    cs = jnp.pad(jnp.cumsum(uf, axis=1), ((0, 0), (1, 0), (0, 0)))
    t = jnp.arange(S)
    groups = []
    for g, w in enumerate(POOL_WINDOWS):
        sl = slice(g * POOL_GROUP, (g + 1) * POOL_GROUP)
        lo = jnp.maximum(t + 1 - w, 0)
        cnt = (t + 1 - lo).astype(jnp.float32)[None, :, None]
        mean = (cs[:, 1:, sl] - cs[:, lo, sl]) / cnt
        groups.append(mean - uf[:, :, sl])
    pooled = jnp.stack(groups, axis=2)
    y = jnp.einsum('bsgc,gcd->bsgd', pooled, pool_w.astype(jnp.float32)).reshape(B, S, POOL_CH)
    return (y * pool_scale.astype(jnp.float32)).astype(u.dtype)


def mix_odd(h, w_in, conv_w, pool_w, pool_scale, w_o):
    proj = h @ w_in
    bg, cg, xc, u = _split(proj, ODD_SPLITS)
    oc = bg * causal_dwconv(cg * xc, conv_w)
    od = multiscale_pool(u, pool_w, pool_scale)
    return jnp.concatenate([oc, od], axis=-1) @ w_o


def mlp(h, w1, w2):
    a = jax.nn.relu(h @ w1)
    return (a * a) @ w2


def setup_inputs(seed: int = 0) -> dict:
    key = jax.random.key(seed)
    ks = jax.random.split(key, 16)
    f32 = jnp.float32

    def nrm(k, shape, fan_in):
        return jax.random.normal(k, shape, f32) * (fan_in ** -0.5)

    def gain(k, shape):
        return 1.0 + 0.02 * jax.random.normal(k, shape, f32)

    return {
        "x": jax.random.normal(ks[0], (BATCH, SEQ, D_MODEL), f32),
        "norm_mix": gain(ks[1], (DEPTH, D_MODEL)),
        "norm_mlp": gain(ks[2], (DEPTH, D_MODEL)),
        "w_o": nrm(ks[3], (DEPTH, D_MODEL, D_MODEL), D_MODEL),
        "w1": nrm(ks[4], (DEPTH, D_MODEL, D_FF), D_MODEL),
        "w2": nrm(ks[5], (DEPTH, D_FF, D_MODEL), D_FF),
        "even_w_in": nrm(ks[6], (N_EVEN, D_MODEL, EVEN_IN), D_MODEL),
        "gla_w_gate2": nrm(ks[7], (N_EVEN, GLA_GATE_RANK, GLA_QK), GLA_GATE_RANK),
        "gla_b_gate": 0.1 * jax.random.normal(ks[8], (N_EVEN, GLA_QK), f32),
        "gla_out_norm": gain(ks[9], (N_EVEN, GLA_DV)),
        "moba_q_norm": gain(ks[10], (N_EVEN, MOBA_DH)),
        "moba_k_norm": gain(ks[11], (N_EVEN, MOBA_DH)),
        "odd_w_in": nrm(ks[12], (N_ODD, D_MODEL, ODD_IN), D_MODEL),
        "conv_w": nrm(ks[13], (N_ODD, CONV_K, CONV_CH), CONV_K),
        "pool_w": nrm(ks[14], (N_ODD, len(POOL_WINDOWS), POOL_GROUP, POOL_GROUP), POOL_GROUP),
        "pool_scale": gain(ks[15], (N_ODD, POOL_CH)),
    }


def reference(x, norm_mix, norm_mlp, w_o, w1, w2, even_w_in, gla_w_gate2, gla_b_gate,
              gla_out_norm, moba_q_norm, moba_k_norm, odd_w_in, conv_w, pool_w, pool_scale):
    for i in range(DEPTH):
        j = i // 2
        h = rmsnorm(x, norm_mix[i])
        if i % 2 == 0:
            x = x + mix_even(h, even_w_in[j], gla_w_gate2[j], gla_b_gate[j], gla_out_norm[j],
                             moba_q_norm[j], moba_k_norm[j], w_o[i])
        else:
            x = x + mix_odd(h, odd_w_in[j], conv_w[j], pool_w[j], pool_scale[j], w_o[i])
        h = rmsnorm(x, norm_mlp[i])
        x = x + mlp(h, w1[i], w2[i])
    return x
```

```python
import functools

import numpy as np
import jax
import jax.numpy as jnp
from jax import lax
from jax.experimental import pallas as pl
from jax.experimental.pallas import tpu as pltpu

F32 = jnp.float32
BF16 = jnp.bfloat16

D_MODEL = 1024
GLA_HEADS = 4
GLA_DK = 64
GLA_DV = 128
GLA_GATE_RANK = 16
GLA_GATE_TEMP = 16.0
GLA_CHUNK = 64
MOBA_HEADS = 8
MOBA_DH = 64
MOBA_BLOCK = 256
MOBA_TOPK = 3
CONV_CH = 512
CONV_K = 3
POOL_CH = 512
POOL_WINDOWS = (2, 4, 8, 16)
POOL_GROUP = 128
D_FF = 4096
EPS = 1e-6

GLA_QK = GLA_HEADS * GLA_DK
GLA_V = GLA_HEADS * GLA_DV
MOBA_W = MOBA_HEADS * MOBA_DH
LANES = 128
HALO = 16
NEG = -1e30

GLA_F32_W = GLA_QK + GLA_QK + GLA_V + LANES
EVEN_N = GLA_F32_W + GLA_V + 3 * MOBA_W

VMEM_LIMIT = 48 * 1024 * 1024

NT_DIMS = (((1,), (1,)), ((), ()))
TN_DIMS = (((0,), (0,)), ((), ()))


def _dot(a, b):
    return jnp.dot(a, b, preferred_element_type=F32)


def _dot_nt(a, b):
    return lax.dot_general(a, b, NT_DIMS, preferred_element_type=F32)


def _dot_tn(a, b):
    return lax.dot_general(a, b, TN_DIMS, preferred_element_type=F32)


def _rmsnorm_rows(x, g):
    ms = jnp.mean(x * x, axis=-1, keepdims=True)
    return x * lax.rsqrt(ms + EPS) * g


def _split2(y):
    hi = y.astype(BF16)
    lo = (y - hi.astype(F32)).astype(BF16)
    return hi, lo


def _even_inproj_kernel(x_ref, g_ref, w_ref, gq_ref, gk_ref, grp_ref,
                        gla_ref, va_ref, qb_ref, kb_ref, vb_ref):
    h = _rmsnorm_rows(x_ref[...], g_ref[...]).astype(BF16)
    c0 = GLA_F32_W
    c1 = c0 + GLA_V
    c2 = c1 + MOBA_W
    c3 = c2 + MOBA_W
    gla_ref[...] = _dot(h, w_ref[:, 0:c0])
    va_ref[...] = _dot(h, w_ref[:, c0:c1]).astype(BF16)
    vb_ref[...] = _dot(h, w_ref[:, c3:c3 + MOBA_W]).astype(BF16)

    def headnorm(y, gain):
        hi, lo = _split2(y * y)
        ms = _dot(hi, grp_ref[...]) + _dot(lo, grp_ref[...])
        return (y * lax.rsqrt(ms + EPS) * gain).astype(BF16)

    qb_ref[...] = headnorm(_dot(h, w_ref[:, c1:c2]), gq_ref[...])
    kb_ref[...] = headnorm(_dot(h, w_ref[:, c2:c3]), gk_ref[...])


def _even_inproj(x2, g, w, gq, gk, grp, tm):
    m = x2.shape[0]
    const = lambda i: (0, 0)
    row = lambda i: (i, 0)
    return pl.pallas_call(
        _even_inproj_kernel,
        grid=(m // tm,),
        in_specs=[
            pl.BlockSpec((tm, D_MODEL), row),
            pl.BlockSpec((1, D_MODEL), const),
            pl.BlockSpec((D_MODEL, EVEN_N), const),
            pl.BlockSpec((1, MOBA_W), const),
            pl.BlockSpec((1, MOBA_W), const),
            pl.BlockSpec((MOBA_W, MOBA_W), const),
        ],
        out_specs=[
            pl.BlockSpec((tm, GLA_F32_W), row),
            pl.BlockSpec((tm, GLA_V), row),
            pl.BlockSpec((tm, MOBA_W), row),
            pl.BlockSpec((tm, MOBA_W), row),
            pl.BlockSpec((tm, MOBA_W), row),
        ],
        out_shape=[
            jax.ShapeDtypeStruct((m, GLA_F32_W), F32),
            jax.ShapeDtypeStruct((m, GLA_V), BF16),
            jax.ShapeDtypeStruct((m, MOBA_W), BF16),
            jax.ShapeDtypeStruct((m, MOBA_W), BF16),
            jax.ShapeDtypeStruct((m, MOBA_W), BF16),
        ],
        compiler_params=pltpu.CompilerParams(
            dimension_semantics=("parallel",), vmem_limit_bytes=VMEM_LIMIT),
        name="even_inproj",
    )(x2, g, w, gq, gk, grp)


def _gla_kernel(gla_ref, va_ref, wg_ref, bg_ref, gn_ref, tri_ref, o_ref, state_ref, *, tile):
    n_chunks = tile // GLA_CHUNK

    @pl.when(pl.program_id(1) == 0)
    def _():
        state_ref[...] = jnp.zeros_like(state_ref)

    r0 = 2 * GLA_QK
    glr = gla_ref[:, r0 + GLA_V:r0 + GLA_V + LANES].astype(BF16)
    pre = _dot(glr, wg_ref[...]) + bg_ref[...]
    log_a = (jnp.minimum(pre, 0.0) - jnp.log1p(jnp.exp(-jnp.abs(pre)))) * (1.0 / GLA_GATE_TEMP)

    g1 = log_a.astype(BF16)
    rem = log_a - g1.astype(F32)
    g2 = rem.astype(BF16)
    g3 = (rem - g2.astype(F32)).astype(BF16)
    tri = tri_ref[...]
    bb = _dot(tri, g1) + _dot(tri, g2) + _dot(tri, g3)
    b = bb[0:tile]
    btot = bb[tile:2 * tile]

    q_ = gla_ref[:, 0:GLA_QK] * (GLA_DK ** -0.5) * jnp.exp(b)
    k = gla_ref[:, GLA_QK:2 * GLA_QK]
    k_ = k * jnp.exp(-b)
    k_end = k * jnp.exp(btot - b)
    causal = tri[0:tile] > 0
    lane = lax.broadcasted_iota(jnp.int32, (tile, LANES), 1)

    for pair in range(GLA_HEADS // 2):
        ls = slice(pair * LANES, (pair + 1) * LANES)
        qp = q_[:, ls]
        kp = k_[:, ls].astype(BF16)
        kep = k_end[:, ls]
        btp = btot[:, ls]
        for sub in range(2):
            head = 2 * pair + sub
            hs = slice(head * GLA_DV, (head + 1) * GLA_DV)
            in_head = (lane < GLA_DK) if sub == 0 else (lane >= GLA_DK)
            qm = jnp.where(in_head, qp, 0.0).astype(BF16)
            kem = jnp.where(in_head, kep, 0.0).astype(BF16)
            v_h = va_ref[:, hs]
            att = jnp.where(causal, _dot_nt(qm, kp), 0.0).astype(BF16)
            o_intra = _dot(att, v_h)
            st = state_ref[head]
            inter = []
            for c in range(n_chunks):
                rows = slice(c * GLA_CHUNK, (c + 1) * GLA_CHUNK)
                inter.append(_dot_nt(qm[rows], st.astype(BF16)))
                decay = jnp.exp(btp[c * GLA_CHUNK:c * GLA_CHUNK + 1, :])
                st = st * decay + _dot_tn(v_h[rows], kem[rows])
            state_ref[head] = st
            o = o_intra + jnp.concatenate(inter, axis=0)
            on = _rmsnorm_rows(o, gn_ref[...])
            r = gla_ref[:, r0 + head * GLA_DV:r0 + (head + 1) * GLA_DV]
            o_ref[:, hs] = (on * (r * jax.nn.sigmoid(r))).astype(BF16)


def _gla(gla_in, va, wg, bg, gn, tri, batch, seq, tile):
    nt = seq // tile
    const = lambda b, t: (0, 0)
    row = lambda b, t: (b * nt + t, 0)
    return pl.pallas_call(
        functools.partial(_gla_kernel, tile=tile),
        grid=(batch, nt),
        in_specs=[
            pl.BlockSpec((tile, GLA_F32_W), row),
            pl.BlockSpec((tile, GLA_V), row),
            pl.BlockSpec((LANES, GLA_QK), const),
            pl.BlockSpec((1, GLA_QK), const),
            pl.BlockSpec((1, GLA_DV), const),
            pl.BlockSpec((2 * tile, tile), const),
        ],
        out_specs=pl.BlockSpec((tile, GLA_V), row),
        out_shape=jax.ShapeDtypeStruct((batch * seq, GLA_V), BF16),
        scratch_shapes=[pltpu.VMEM((GLA_HEADS, GLA_DV, LANES), F32)],
        compiler_params=pltpu.CompilerParams(
            dimension_semantics=("parallel", "arbitrary"), vmem_limit_bytes=VMEM_LIMIT),
        name="gla",
    )(gla_in, va, wg, bg, gn, tri)


def _moba_kernel(slopes_ref, q_ref, k_ref, v_ref, kaux_ref, eye_ref, o_ref, kaug_ref, kmean_ref,
                 *, n_blocks):
    bs = MOBA_BLOCK
    hp = pl.program_id(1)
    i = pl.program_id(2)

    @pl.when(i == 0)
    def _():
        kaug_ref[:, 0:LANES] = k_ref[...]
        kaug_ref[:, LANES:2 * LANES] = kaux_ref[...]
        for j in range(n_blocks):
            blk = k_ref[j * bs:(j + 1) * bs, :].astype(F32)
            kmean_ref[j:j + 1, :] = jnp.sum(blk, axis=0, keepdims=True) * (1.0 / bs)

    q = q_ref[...]
    lane = lax.broadcasted_iota(jnp.int32, (bs, LANES), 1)
    rowf = lax.broadcasted_iota(jnp.int32, (bs, LANES), 0).astype(F32)
    km_hi, km_lo = _split2(kmean_ref[...])
    jrow = lax.broadcasted_iota(jnp.int32, (n_blocks, bs), 0)
    s_row = lax.broadcasted_iota(jnp.int32, (bs, bs), 0)
    s_col = lax.broadcasted_iota(jnp.int32, (bs, bs), 1)
    own_off = pl.multiple_of(i * bs, bs)

    outs = []
    for sub in range(2):
        slope = slopes_ref[2 * hp + sub]
        in_head = (lane < MOBA_DH) if sub == 0 else (lane >= MOBA_DH)
        qm = jnp.where(in_head, q, jnp.zeros_like(q))

        gate = _dot_nt(km_hi, qm) + _dot_nt(km_lo, qm)
        gate = jnp.where(jrow < i, gate, -jnp.inf)
        rank = jnp.zeros((n_blocks, bs), jnp.int32)
        for jp in range(n_blocks):
            g_jp = gate[jp:jp + 1, :]
            tie = jnp.where(g_jp == gate, jnp.where(jrow > jp, 1, 0), 0)
            rank = rank + jnp.where(g_jp > gate, 1, tie)
        rank = jnp.where(jrow < i, rank, jnp.where(jrow == i, 0, MOBA_TOPK))
        keep = rank < MOBA_TOPK
        dist_blocks = (i - jrow).astype(F32) * float(bs)
        bias = jnp.where(keep, -slope * dist_blocks, NEG).astype(BF16)
        bias = jnp.concatenate([bias, jnp.zeros((LANES - n_blocks, bs), BF16)], axis=0)
        bias_t = _dot_nt(eye_ref[...], bias)
        qaux = jnp.where(lane == n_blocks, -slope * rowf,
                         jnp.where(lane == n_blocks + 1, slope, bias_t))
        q_aug = jnp.concatenate([qm, qaux.astype(BF16)], axis=1)

        s = _dot_nt(q_aug, kaug_ref[pl.ds(own_off, bs), :])
        s = jnp.where(s_col <= s_row, s, NEG)
        m0 = jnp.max(s, axis=-1, keepdims=True)
        p = jnp.exp(s - m0)
        l0 = jnp.sum(p, axis=-1, keepdims=True)
        acc0 = _dot(p.astype(BF16), v_ref[pl.ds(own_off, bs), :])

        def body(j, carry):
            m, l, acc = carry
            off = pl.multiple_of(j * bs, bs)
            s = _dot_nt(q_aug, kaug_ref[pl.ds(off, bs), :])
            m_new = jnp.maximum(m, jnp.max(s, axis=-1, keepdims=True))
            alpha = jnp.exp(m - m_new)
            p = jnp.exp(s - m_new)
            l = alpha * l + jnp.sum(p, axis=-1, keepdims=True)
            acc = alpha * acc + _dot(p.astype(BF16), v_ref[pl.ds(off, bs), :])
            return m_new, l, acc

        _, l, acc = lax.fori_loop(0, i, body, (m0, l0, acc0))
        outs.append(acc / l)

    o_ref[...] = jnp.where(lane < MOBA_DH, outs[0], outs[1]).astype(o_ref.dtype)


def _moba(slopes, q, k, v, kaux, eye, batch, seq):
    nb = seq // MOBA_BLOCK
    n_pairs = MOBA_HEADS // 2
    return pl.pallas_call(
        functools.partial(_moba_kernel, n_blocks=nb),
        grid_spec=pltpu.PrefetchScalarGridSpec(
            num_scalar_prefetch=1,
            grid=(batch, n_pairs, nb),
            in_specs=[
                pl.BlockSpec((MOBA_BLOCK, LANES), lambda b, h, i, s: (b * nb + i, h)),
                pl.BlockSpec((seq, LANES), lambda b, h, i, s: (b, h)),
                pl.BlockSpec((seq, LANES), lambda b, h, i, s: (b, h)),
                pl.BlockSpec((seq, LANES), lambda b, h, i, s: (0, 0)),
                pl.BlockSpec((MOBA_BLOCK, MOBA_BLOCK), lambda b, h, i, s: (0, 0)),
            ],
            out_specs=pl.BlockSpec((MOBA_BLOCK, LANES), lambda b, h, i, s: (b * nb + i, h)),
            scratch_shapes=[
                pltpu.VMEM((seq, 2 * LANES), BF16),
                pltpu.VMEM((nb, LANES), F32),
            ],
        ),
        out_shape=jax.ShapeDtypeStruct((batch * seq, MOBA_W), BF16),
        compiler_params=pltpu.CompilerParams(
            dimension_semantics=("parallel", "parallel", "arbitrary"), vmem_limit_bytes=VMEM_LIMIT),
        name="moba",
    )(slopes, q, k, v, kaux, eye)


def _outproj_kernel(x_ref, oa_ref, ob_ref, w_ref, o_ref):
    o_ref[...] = (x_ref[...] + _dot(oa_ref[...], w_ref[0:GLA_V, :])
                  + _dot(ob_ref[...], w_ref[GLA_V:GLA_V + MOBA_W, :]))


def _outproj(x2, oa, ob, w, tm):
    m = x2.shape[0]
    row = lambda i: (i, 0)
    return pl.pallas_call(
        _outproj_kernel,
        grid=(m // tm,),
        in_specs=[
            pl.BlockSpec((tm, D_MODEL), row),
            pl.BlockSpec((tm, GLA_V), row),
            pl.BlockSpec((tm, MOBA_W), row),
            pl.BlockSpec((D_MODEL, D_MODEL), lambda i: (0, 0)),
        ],
        out_specs=pl.BlockSpec((tm, D_MODEL), row),
        out_shape=jax.ShapeDtypeStruct((m, D_MODEL), F32),
        compiler_params=pltpu.CompilerParams(
            dimension_semantics=("parallel",), vmem_limit_bytes=VMEM_LIMIT),
        name="outproj",
    )(x2, oa, ob, w)


def _mlp_kernel(x_ref, g_ref, w1_ref, w2_ref, o_ref, h_ref, acc_ref):
    f = pl.program_id(1)

    @pl.when(f == 0)
    def _():
        h_ref[...] = _rmsnorm_rows(x_ref[...], g_ref[...]).astype(BF16)
        acc_ref[...] = x_ref[...]

    a = jnp.maximum(_dot(h_ref[...], w1_ref[...]), 0.0)
    acc_ref[...] += _dot((a * a).astype(BF16), w2_ref[...])

    @pl.when(f == pl.num_programs(1) - 1)
    def _():
        o_ref[...] = acc_ref[...]


def _mlp(x2, g, w1, w2, tm, tf):
    m = x2.shape[0]
    return pl.pallas_call(
        _mlp_kernel,
        grid=(m // tm, D_FF // tf),
        in_specs=[
            pl.BlockSpec((tm, D_MODEL), lambda i, f: (i, 0)),
            pl.BlockSpec((1, D_MODEL), lambda i, f: (0, 0)),
            pl.BlockSpec((D_MODEL, tf), lambda i, f: (0, f)),
            pl.BlockSpec((tf, D_MODEL), lambda i, f: (f, 0)),
        ],
        out_specs=pl.BlockSpec((tm, D_MODEL), lambda i, f: (i, 0)),
        out_shape=jax.ShapeDtypeStruct((m, D_MODEL), F32),
        scratch_shapes=[pltpu.VMEM((tm, D_MODEL), BF16), pltpu.VMEM((tm, D_MODEL), F32)],
        compiler_params=pltpu.CompilerParams(
            dimension_semantics=("parallel", "arbitrary"), vmem_limit_bytes=VMEM_LIMIT),
        name="mlp",
    )(x2, g, w1, w2)


def _odd_kernel(x_ref, g_ref, win_ref, cw_ref, pw_ref, ps_ref, wo_ref, o_ref,
                zbuf_ref, ubuf_ref, mix_ref, *, tile):
    t_idx = pl.program_id(1)

    @pl.when(t_idx == 0)
    def _():
        zbuf_ref[0:HALO, :] = jnp.zeros((HALO, CONV_CH), F32)
        ubuf_ref[0:HALO, :] = jnp.zeros((HALO, POOL_CH), F32)

    x = x_ref[...]
    h = _rmsnorm_rows(x, g_ref[...]).astype(BF16)
    bg = _dot(h, win_ref[:, 0:CONV_CH])
    cg = _dot(h, win_ref[:, CONV_CH:2 * CONV_CH])
    xc = _dot(h, win_ref[:, 2 * CONV_CH:3 * CONV_CH])
    u = _dot(h, win_ref[:, 3 * CONV_CH:3 * CONV_CH + POOL_CH])

    z = cg * xc
    zbuf_ref[HALO:HALO + tile, :] = z
    y = z * cw_ref[CONV_K - 1:CONV_K, :]
    for tap in range(CONV_K - 1):
        shift = CONV_K - 1 - tap
        y = y + zbuf_ref[HALO - shift:HALO - shift + tile, :] * cw_ref[tap:tap + 1, :]
    mix_ref[:, 0:CONV_CH] = (bg * y).astype(BF16)
    zbuf_ref[0:HALO, :] = zbuf_ref[tile:tile + HALO, :]

    ubuf_ref[HALO:HALO + tile, :] = u
    pos = lax.broadcasted_iota(jnp.int32, (tile, POOL_GROUP), 0) + t_idx * tile
    for grp, win in enumerate(POOL_WINDOWS):
        cs = slice(grp * POOL_GROUP, (grp + 1) * POOL_GROUP)
        u_g = u[:, cs]
        tot = u_g
        for shift in range(1, win):
            tot = tot + ubuf_ref[HALO - shift:HALO - shift + tile, cs]
        cnt = jnp.minimum(pos + 1, win).astype(F32)
        pooled = (tot / cnt - u_g).astype(BF16)
        y_g = _dot(pooled, pw_ref[grp]) * ps_ref[:, cs]
        mix_ref[:, CONV_CH + grp * POOL_GROUP:CONV_CH + (grp + 1) * POOL_GROUP] = y_g.astype(BF16)
    ubuf_ref[0:HALO, :] = ubuf_ref[tile:tile + HALO, :]

    o_ref[...] = x + _dot(mix_ref[...], wo_ref[...])


def _odd_mix(x2, g, win, cw, pw, ps, wo, batch, seq, tile):
    nt = seq // tile
    const = lambda b, t: (0, 0)
    row = lambda b, t: (b * nt + t, 0)
    n_in = 3 * CONV_CH + POOL_CH
    return pl.pallas_call(
        functools.partial(_odd_kernel, tile=tile),
        grid=(batch, nt),
        in_specs=[
            pl.BlockSpec((tile, D_MODEL), row),
            pl.BlockSpec((1, D_MODEL), const),
            pl.BlockSpec((D_MODEL, n_in), const),
            pl.BlockSpec((CONV_K, CONV_CH), const),
            pl.BlockSpec((len(POOL_WINDOWS), POOL_GROUP, POOL_GROUP), lambda b, t: (0, 0, 0)),
            pl.BlockSpec((1, POOL_CH), const),
            pl.BlockSpec((D_MODEL, D_MODEL), const),
        ],
        out_specs=pl.BlockSpec((tile, D_MODEL), row),
        out_shape=jax.ShapeDtypeStruct((batch * seq, D_MODEL), F32),
        scratch_shapes=[
            pltpu.VMEM((tile + HALO, CONV_CH), F32),
            pltpu.VMEM((tile + HALO, POOL_CH), F32),
            pltpu.VMEM((tile, CONV_CH + POOL_CH), BF16),
        ],
        compiler_params=pltpu.CompilerParams(
            dimension_semantics=("parallel", "arbitrary"), vmem_limit_bytes=VMEM_LIMIT),
        name="odd_mix",
    )(x2, g, win, cw, pw, ps, wo)


def _gla_tri(tile):
    r = np.arange(tile)
    same = (r[:, None] // GLA_CHUNK) == (r[None, :] // GLA_CHUNK)
    lower = same & (r[None, :] <= r[:, None])
    return jnp.asarray(np.concatenate([lower, same], axis=0).astype(np.float32), dtype=BF16)


def _moba_kaux(seq):
    pos = np.arange(seq)
    nb = seq // MOBA_BLOCK
    aux = np.zeros((seq, LANES), np.float32)
    aux[pos, pos // MOBA_BLOCK] = 1.0
    aux[:, nb] = 1.0
    aux[:, nb + 1] = pos % MOBA_BLOCK
    return jnp.asarray(aux, dtype=BF16)


def _head_group_matrix():
    r = np.arange(MOBA_W)
    same = (r[:, None] // MOBA_DH) == (r[None, :] // MOBA_DH)
    return jnp.asarray(same.astype(np.float32) / MOBA_DH, dtype=BF16)


def _even_weight(w_in):
    qa, ka, va, ra, glr, qb, kb, vb = jnp.split(
        w_in, np.cumsum([GLA_QK, GLA_QK, GLA_V, GLA_V, GLA_GATE_RANK, MOBA_W, MOBA_W])[:].tolist(), axis=-1)
    glr_pad = jnp.pad(glr, ((0, 0), (0, LANES - GLA_GATE_RANK)))
    return jnp.concatenate([qa, ka, ra, glr_pad, va, qb, kb, vb], axis=-1).astype(BF16)


def kernel(x, norm_mix, norm_mlp, w_o, w1, w2, even_w_in, gla_w_gate2, gla_b_gate, gla_out_norm,
           moba_q_norm, moba_k_norm, odd_w_in, conv_w, pool_w, pool_scale):
    batch, seq, d = x.shape
    assert d == D_MODEL and seq % MOBA_BLOCK == 0
    assert seq // MOBA_BLOCK + 2 <= LANES and seq // MOBA_BLOCK % 8 == 0
    depth = norm_mix.shape[0]
    m = batch * seq
    tm = 512
    gla_tile = 256
    odd_tile = 512

    tri = _gla_tri(gla_tile)
    kaux = _moba_kaux(seq)
    eye = jnp.eye(MOBA_BLOCK, dtype=BF16)
    grp = _head_group_matrix()
    slopes = jnp.exp2(-8.0 * jnp.arange(1, MOBA_HEADS + 1, dtype=F32) / MOBA_HEADS)

    x2 = x.reshape(m, d)
    for i in range(depth):
        j = i // 2
        g_mix = norm_mix[i].reshape(1, d)
        wo = w_o[i].astype(BF16)
        if i % 2 == 0:
            w_in = _even_weight(even_w_in[j])
            gq = (jnp.tile(moba_q_norm[j], MOBA_HEADS) * (MOBA_DH ** -0.5)).reshape(1, MOBA_W)
            gk = jnp.tile(moba_k_norm[j], MOBA_HEADS).reshape(1, MOBA_W)
            gla_in, va, qb, kb, vb = _even_inproj(x2, g_mix, w_in, gq, gk, grp, tm)
            wg = jnp.pad(gla_w_gate2[j], ((0, LANES - GLA_GATE_RANK), (0, 0))).astype(BF16)
            oa = _gla(gla_in, va, wg, gla_b_gate[j].reshape(1, GLA_QK),
                      gla_out_norm[j].reshape(1, GLA_DV), tri, batch, seq, gla_tile)
            ob = _moba(slopes, qb, kb, vb, kaux, eye, batch, seq)
            x2 = _outproj(x2, oa, ob, wo, tm)
        else:
            x2 = _odd_mix(x2, g_mix, odd_w_in[j].astype(BF16), conv_w[j], pool_w[j].astype(BF16),
                          pool_scale[j].reshape(1, POOL_CH), wo, batch, seq, odd_tile)
        x2 = _mlp(x2, norm_mlp[i].reshape(1, d), w1[i].astype(BF16), w2[i].astype(BF16), 1024, 512)
    return x2.reshape(batch, seq, d)
```

```python
import functools

import numpy as np
import jax
import jax.numpy as jnp
from jax import lax
from jax.experimental import pallas as pl
from jax.experimental.pallas import tpu as pltpu

F32 = jnp.float32
BF16 = jnp.bfloat16

D_MODEL = 1024
GLA_HEADS = 4
GLA_DK = 64
GLA_DV = 128
GLA_GATE_RANK = 16
GLA_GATE_TEMP = 16.0
GLA_CHUNK = 64
MOBA_HEADS = 8
MOBA_DH = 64
MOBA_BLOCK = 256
MOBA_TOPK = 3
CONV_CH = 512
CONV_K = 3
POOL_CH = 512
POOL_WINDOWS = (2, 4, 8, 16)
POOL_GROUP = 128
D_FF = 4096
EPS = 1e-6

GLA_QK = GLA_HEADS * GLA_DK
GLA_V = GLA_HEADS * GLA_DV
MOBA_W = MOBA_HEADS * MOBA_DH
LANES = 128
HALO = 16
NEG = -1e30

GLA_F32_W = GLA_QK + GLA_QK + GLA_V + LANES
EVEN_N = GLA_F32_W + GLA_V + 3 * MOBA_W

VMEM_LIMIT = 48 * 1024 * 1024

NT_DIMS = (((1,), (1,)), ((), ()))
TN_DIMS = (((0,), (0,)), ((), ()))


def _dot(a, b):
    return jnp.dot(a, b, preferred_element_type=F32)


def _dot_nt(a, b):
    return lax.dot_general(a, b, NT_DIMS, preferred_element_type=F32)


def _dot_tn(a, b):
    return lax.dot_general(a, b, TN_DIMS, preferred_element_type=F32)


def _rmsnorm_rows(x, g):
    ms = jnp.mean(x * x, axis=-1, keepdims=True)
    return x * lax.rsqrt(ms + EPS) * g


def _split2(y):
    hi = y.astype(BF16)
    lo = (y - hi.astype(F32)).astype(BF16)
    return hi, lo


def _even_inproj_kernel(x_ref, g_ref, w_ref, gq_ref, gk_ref, grp_ref,
                        gla_ref, va_ref, qb_ref, kb_ref, vb_ref):
    h = _rmsnorm_rows(x_ref[...], g_ref[...]).astype(BF16)
    c0 = GLA_F32_W
    c1 = c0 + GLA_V
    c2 = c1 + MOBA_W
    c3 = c2 + MOBA_W
    gla_ref[...] = _dot(h, w_ref[:, 0:c0])
    va_ref[...] = _dot(h, w_ref[:, c0:c1]).astype(BF16)
    vb_ref[...] = _dot(h, w_ref[:, c3:c3 + MOBA_W]).astype(BF16)

    def headnorm(y, gain):
        hi, lo = _split2(y * y)
        ms = _dot(hi, grp_ref[...]) + _dot(lo, grp_ref[...])
        return (y * lax.rsqrt(ms + EPS) * gain).astype(BF16)

    qb_ref[...] = headnorm(_dot(h, w_ref[:, c1:c2]), gq_ref[...])
    kb_ref[...] = headnorm(_dot(h, w_ref[:, c2:c3]), gk_ref[...])


def _even_inproj(x2, g, w, gq, gk, grp, tm):
    m = x2.shape[0]
    const = lambda i: (0, 0)
    row = lambda i: (i, 0)
    return pl.pallas_call(
        _even_inproj_kernel,
        grid=(m // tm,),
        in_specs=[
            pl.BlockSpec((tm, D_MODEL), row),
            pl.BlockSpec((1, D_MODEL), const),
            pl.BlockSpec((D_MODEL, EVEN_N), const),
            pl.BlockSpec((1, MOBA_W), const),
            pl.BlockSpec((1, MOBA_W), const),
            pl.BlockSpec((MOBA_W, MOBA_W), const),
        ],
        out_specs=[
            pl.BlockSpec((tm, GLA_F32_W), row),
            pl.BlockSpec((tm, GLA_V), row),
            pl.BlockSpec((tm, MOBA_W), row),
            pl.BlockSpec((tm, MOBA_W), row),
            pl.BlockSpec((tm, MOBA_W), row),
        ],
        out_shape=[
            jax.ShapeDtypeStruct((m, GLA_F32_W), F32),
            jax.ShapeDtypeStruct((m, GLA_V), BF16),
            jax.ShapeDtypeStruct((m, MOBA_W), BF16),
            jax.ShapeDtypeStruct((m, MOBA_W), BF16),
            jax.ShapeDtypeStruct((m, MOBA_W), BF16),
        ],
        compiler_params=pltpu.CompilerParams(
            dimension_semantics=("parallel",), vmem_limit_bytes=VMEM_LIMIT),
        name="even_inproj",
    )(x2, g, w, gq, gk, grp)


def _gla_kernel(gla_ref, va_ref, wg_ref, bg_ref, gn_ref, tri_ref, o_ref, state_ref, *, tile):
    n_chunks = tile // GLA_CHUNK

    @pl.when(pl.program_id(1) == 0)
    def _():
        state_ref[...] = jnp.zeros_like(state_ref)

    r0 = 2 * GLA_QK
    glr = gla_ref[:, r0 + GLA_V:r0 + GLA_V + LANES].astype(BF16)
    pre = _dot(glr, wg_ref[...]) + bg_ref[...]
    log_a = (jnp.minimum(pre, 0.0) - jnp.log1p(jnp.exp(-jnp.abs(pre)))) * (1.0 / GLA_GATE_TEMP)

    g1 = log_a.astype(BF16)
    rem = log_a - g1.astype(F32)
    g2 = rem.astype(BF16)
    g3 = (rem - g2.astype(F32)).astype(BF16)
    tri = tri_ref[...]
    bb = _dot(tri, g1) + _dot(tri, g2) + _dot(tri, g3)
    b = bb[0:tile]
    btot = bb[tile:2 * tile]

    q_ = gla_ref[:, 0:GLA_QK] * (GLA_DK ** -0.5) * jnp.exp(b)
    k = gla_ref[:, GLA_QK:2 * GLA_QK]
    k_ = k * jnp.exp(-b)
    k_end = k * jnp.exp(btot - b)
    causal = tri[0:tile] > 0
    lane = lax.broadcasted_iota(jnp.int32, (tile, LANES), 1)

    for pair in range(GLA_HEADS // 2):
        ls = slice(pair * LANES, (pair + 1) * LANES)
        qp = q_[:, ls]
        kp = k_[:, ls].astype(BF16)
        kep = k_end[:, ls]
        btp = btot[:, ls]
        for sub in range(2):
            head = 2 * pair + sub
            hs = slice(head * GLA_DV, (head + 1) * GLA_DV)
            in_head = (lane < GLA_DK) if sub == 0 else (lane >= GLA_DK)
            qm = jnp.where(in_head, qp, 0.0).astype(BF16)
            kem = jnp.where(in_head, kep, 0.0).astype(BF16)
            v_h = va_ref[:, hs]
            att = jnp.where(causal, _dot_nt(qm, kp), 0.0).astype(BF16)
            o_intra = _dot(att, v_h)
            st = state_ref[head]
            inter = []
            for c in range(n_chunks):
                rows = slice(c * GLA_CHUNK, (c + 1) * GLA_CHUNK)
                inter.append(_dot_nt(qm[rows], st.astype(BF16)))
                decay = jnp.exp(btp[c * GLA_CHUNK:c * GLA_CHUNK + 1, :])
                st = st * decay + _dot_tn(v_h[rows], kem[rows])
            state_ref[head] = st
            o = o_intra + jnp.concatenate(inter, axis=0)
            on = _rmsnorm_rows(o, gn_ref[...])
            r = gla_ref[:, r0 + head * GLA_DV:r0 + (head + 1) * GLA_DV]
            o_ref[:, hs] = (on * (r * jax.nn.sigmoid(r))).astype(BF16)


def _gla(gla_in, va, wg, bg, gn, tri, batch, seq, tile):
    nt = seq // tile
    const = lambda b, t: (0, 0)
    row = lambda b, t: (b * nt + t, 0)
    return pl.pallas_call(
        functools.partial(_gla_kernel, tile=tile),
        grid=(batch, nt),
        in_specs=[
            pl.BlockSpec((tile, GLA_F32_W), row),
            pl.BlockSpec((tile, GLA_V), row),
            pl.BlockSpec((LANES, GLA_QK), const),
            pl.BlockSpec((1, GLA_QK), const),
            pl.BlockSpec((1, GLA_DV), const),
            pl.BlockSpec((2 * tile, tile), const),
        ],
        out_specs=pl.BlockSpec((tile, GLA_V), row),
        out_shape=jax.ShapeDtypeStruct((batch * seq, GLA_V), BF16),
        scratch_shapes=[pltpu.VMEM((GLA_HEADS, GLA_DV, LANES), F32)],
        compiler_params=pltpu.CompilerParams(
            dimension_semantics=("parallel", "arbitrary"), vmem_limit_bytes=VMEM_LIMIT),
        name="gla",
    )(gla_in, va, wg, bg, gn, tri)


def _moba_kernel(slopes_ref, q_ref, k_ref, v_ref, kaux_ref, eye_ref, o_ref,
                 kaug_ref, vt_ref, qaug_ref, km_ref, *, n_blocks):
    bs = MOBA_BLOCK
    nb = n_blocks
    pair_keys = 2 * bs
    hp = pl.program_id(1)
    eye = eye_ref[...]
    lane1 = lax.broadcasted_iota(jnp.int32, (1, LANES), 1)

    for jj in range(nb // 2):
        rows = slice(jj * pair_keys, (jj + 1) * pair_keys)
        kaug_ref[jj, :, 0:LANES] = k_ref[rows, :]
        kaug_ref[jj, :, LANES:2 * LANES] = kaux_ref[rows, :]
        vt_ref[jj] = jnp.transpose(v_ref[rows, :])
    for j in range(nb):
        km = jnp.sum(k_ref[j * bs:(j + 1) * bs, :].astype(F32), axis=0, keepdims=True) * (1.0 / bs)
        km_ref[j:j + 1, :] = jnp.where(lane1 < MOBA_DH, km, 0.0)
        km_ref[nb + j:nb + j + 1, :] = jnp.where(lane1 >= MOBA_DH, km, 0.0)
    km_hi, km_lo = _split2(km_ref[...])

    lane = lax.broadcasted_iota(jnp.int32, (bs, LANES), 1)
    rowf = lax.broadcasted_iota(jnp.int32, (bs, LANES), 0).astype(F32)
    jrow = lax.broadcasted_iota(jnp.int32, (nb, bs), 0)

    def prepare_queries(i, carry):
        q = q_ref[pl.ds(pl.multiple_of(i * bs, bs), bs), :]
        gate_both = _dot_nt(km_hi, q) + _dot_nt(km_lo, q)
        for sub in range(2):
            slope = slopes_ref[2 * hp + sub]
            gate = jnp.where(jrow < i, gate_both[sub * nb:(sub + 1) * nb], -jnp.inf)
            rank = jnp.zeros((nb, bs), jnp.int32)
            for jp in range(nb):
                g_jp = gate[jp:jp + 1, :]
                tie = jnp.where(g_jp == gate, jnp.where(jrow > jp, 1, 0), 0)
                rank = rank + jnp.where(g_jp > gate, 1, tie)
            rank = jnp.where(jrow < i, rank, jnp.where(jrow == i, 0, MOBA_TOPK))
            dist_blocks = (i - jrow).astype(F32) * float(bs)
            bias = jnp.where(rank < MOBA_TOPK, -slope * dist_blocks, NEG).astype(BF16)
            bias = jnp.concatenate([bias, jnp.zeros((LANES - nb, bs), BF16)], axis=0)
            bias_t = _dot_nt(eye, bias)
            qaux = jnp.where(lane == nb, -slope * rowf, jnp.where(lane == nb + 1, slope, bias_t))
            in_head = (lane < MOBA_DH) if sub == 0 else (lane >= MOBA_DH)
            qaug_ref[sub, i, :, 0:LANES] = jnp.where(in_head, q, jnp.zeros_like(q))
            qaug_ref[sub, i, :, LANES:2 * LANES] = qaux.astype(BF16)
        return carry

    lax.fori_loop(0, nb, prepare_queries, 0)

    key_minus_query = (lax.broadcasted_iota(jnp.int32, (pair_keys, bs), 0)
                       - lax.broadcasted_iota(jnp.int32, (pair_keys, bs), 1))
    drow = lax.broadcasted_iota(jnp.int32, (LANES, bs), 0)

    def attend(i, carry):
        qa = [qaug_ref[sub, i] for sub in range(2)]
        own = i // 2
        limit = (i - 2 * own) * bs
        kj = kaug_ref[own]
        vj = vt_ref[own]
        state = []
        for sub in range(2):
            s = jnp.where(key_minus_query <= limit, _dot_nt(kj, qa[sub]), NEG)
            m = jnp.max(s, axis=0, keepdims=True)
            p = jnp.exp(s - m)
            state += [m, jnp.sum(p, axis=0, keepdims=True), _dot(vj, p.astype(BF16))]

        def past_pair(jj, st):
            kj = kaug_ref[jj]
            vj = vt_ref[jj]
            new = []
            for sub in range(2):
                m, l, acc = st[3 * sub:3 * sub + 3]
                s = _dot_nt(kj, qa[sub])
                m_new = jnp.maximum(m, jnp.max(s, axis=0, keepdims=True))
                alpha = jnp.exp(m - m_new)
                p = jnp.exp(s - m_new)
                l = alpha * l + jnp.sum(p, axis=0, keepdims=True)
                acc = alpha * acc + _dot(vj, p.astype(BF16))
                new += [m_new, l, acc]
            return tuple(new)

        st = lax.fori_loop(0, own, past_pair, tuple(state))
        out_t = jnp.where(drow < MOBA_DH, st[2] / st[1], st[5] / st[4]).astype(BF16)
        o_ref[pl.ds(pl.multiple_of(i * bs, bs), bs), :] = _dot_nt(eye, out_t).astype(o_ref.dtype)
        return carry

    lax.fori_loop(0, nb, attend, 0)


def _moba(slopes, q, k, v, kaux, eye, batch, seq):
    nb = seq // MOBA_BLOCK
    n_pairs = MOBA_HEADS // 2
    blk = lambda b, h, s: (b, h)
    return pl.pallas_call(
        functools.partial(_moba_kernel, n_blocks=nb),
        grid_spec=pltpu.PrefetchScalarGridSpec(
            num_scalar_prefetch=1,
            grid=(batch, n_pairs),
            in_specs=[
                pl.BlockSpec((seq, LANES), blk),
                pl.BlockSpec((seq, LANES), blk),
                pl.BlockSpec((seq, LANES), blk),
                pl.BlockSpec((seq, LANES), lambda b, h, s: (0, 0)),
                pl.BlockSpec((MOBA_BLOCK, MOBA_BLOCK), lambda b, h, s: (0, 0)),
            ],
            out_specs=pl.BlockSpec((seq, LANES), blk),
            scratch_shapes=[
                pltpu.VMEM((nb // 2, 2 * MOBA_BLOCK, 2 * LANES), BF16),
                pltpu.VMEM((nb // 2, LANES, 2 * MOBA_BLOCK), BF16),
                pltpu.VMEM((2, nb, MOBA_BLOCK, 2 * LANES), BF16),
                pltpu.VMEM((2 * nb, LANES), F32),
            ],
        ),
        out_shape=jax.ShapeDtypeStruct((batch * seq, MOBA_W), BF16),
        compiler_params=pltpu.CompilerParams(
            dimension_semantics=("parallel", "parallel"), vmem_limit_bytes=VMEM_LIMIT),
        name="moba",
    )(slopes, q, k, v, kaux, eye)


def _outproj_kernel(x_ref, oa_ref, ob_ref, w_ref, o_ref):
    o_ref[...] = (x_ref[...] + _dot(oa_ref[...], w_ref[0:GLA_V, :])
                  + _dot(ob_ref[...], w_ref[GLA_V:GLA_V + MOBA_W, :]))


def _outproj(x2, oa, ob, w, tm):
    m = x2.shape[0]
    row = lambda i: (i, 0)
    return pl.pallas_call(
        _outproj_kernel,
        grid=(m // tm,),
        in_specs=[
            pl.BlockSpec((tm, D_MODEL), row),
            pl.BlockSpec((tm, GLA_V), row),
            pl.BlockSpec((tm, MOBA_W), row),
            pl.BlockSpec((D_MODEL, D_MODEL), lambda i: (0, 0)),
        ],
        out_specs=pl.BlockSpec((tm, D_MODEL), row),
        out_shape=jax.ShapeDtypeStruct((m, D_MODEL), F32),
        compiler_params=pltpu.CompilerParams(
            dimension_semantics=("parallel",), vmem_limit_bytes=VMEM_LIMIT),
        name="outproj",
    )(x2, oa, ob, w)


def _mlp_kernel(x_ref, g_ref, w1_ref, w2_ref, o_ref, h_ref, acc_ref):
    f = pl.program_id(1)

    @pl.when(f == 0)
    def _():
        h_ref[...] = _rmsnorm_rows(x_ref[...], g_ref[...]).astype(BF16)
        acc_ref[...] = x_ref[...]

    a = jnp.maximum(_dot(h_ref[...], w1_ref[...]), 0.0)
    acc_ref[...] += _dot((a * a).astype(BF16), w2_ref[...])

    @pl.when(f == pl.num_programs(1) - 1)
    def _():
        o_ref[...] = acc_ref[...]


def _mlp(x2, g, w1, w2, tm, tf):
    m = x2.shape[0]
    return pl.pallas_call(
        _mlp_kernel,
        grid=(m // tm, D_FF // tf),
        in_specs=[
            pl.BlockSpec((tm, D_MODEL), lambda i, f: (i, 0)),
            pl.BlockSpec((1, D_MODEL), lambda i, f: (0, 0)),
            pl.BlockSpec((D_MODEL, tf), lambda i, f: (0, f)),
            pl.BlockSpec((tf, D_MODEL), lambda i, f: (f, 0)),
        ],
        out_specs=pl.BlockSpec((tm, D_MODEL), lambda i, f: (i, 0)),
        out_shape=jax.ShapeDtypeStruct((m, D_MODEL), F32),
        scratch_shapes=[pltpu.VMEM((tm, D_MODEL), BF16), pltpu.VMEM((tm, D_MODEL), F32)],
        compiler_params=pltpu.CompilerParams(
            dimension_semantics=("parallel", "arbitrary"), vmem_limit_bytes=VMEM_LIMIT),
        name="mlp",
    )(x2, g, w1, w2)


def _odd_kernel(x_ref, g_ref, win_ref, cw_ref, pw_ref, ps_ref, wo_ref, o_ref,
                zbuf_ref, ubuf_ref, mix_ref, *, tile):
    t_idx = pl.program_id(1)

    @pl.when(t_idx == 0)
    def _():
        zbuf_ref[0:HALO, :] = jnp.zeros((HALO, CONV_CH), F32)
        ubuf_ref[0:HALO, :] = jnp.zeros((HALO, POOL_CH), F32)

    x = x_ref[...]
    h = _rmsnorm_rows(x, g_ref[...]).astype(BF16)
    bg = _dot(h, win_ref[:, 0:CONV_CH])
    cg = _dot(h, win_ref[:, CONV_CH:2 * CONV_CH])
    xc = _dot(h, win_ref[:, 2 * CONV_CH:3 * CONV_CH])
    u = _dot(h, win_ref[:, 3 * CONV_CH:3 * CONV_CH + POOL_CH])

    z = cg * xc
    zbuf_ref[HALO:HALO + tile, :] = z
    y = z * cw_ref[CONV_K - 1:CONV_K, :]
    for tap in range(CONV_K - 1):
        shift = CONV_K - 1 - tap
        y = y + zbuf_ref[HALO - shift:HALO - shift + tile, :] * cw_ref[tap:tap + 1, :]
    mix_ref[:, 0:CONV_CH] = (bg * y).astype(BF16)
    zbuf_ref[0:HALO, :] = zbuf_ref[tile:tile + HALO, :]

    ubuf_ref[HALO:HALO + tile, :] = u
    pos = lax.broadcasted_iota(jnp.int32, (tile, POOL_GROUP), 0) + t_idx * tile
    for grp, win in enumerate(POOL_WINDOWS):
        cs = slice(grp * POOL_GROUP, (grp + 1) * POOL_GROUP)
        u_g = u[:, cs]
        tot = u_g
        for shift in range(1, win):
            tot = tot + ubuf_ref[HALO - shift:HALO - shift + tile, cs]
        cnt = jnp.minimum(pos + 1, win).astype(F32)
        pooled = (tot / cnt - u_g).astype(BF16)
        y_g = _dot(pooled, pw_ref[grp]) * ps_ref[:, cs]
        mix_ref[:, CONV_CH + grp * POOL_GROUP:CONV_CH + (grp + 1) * POOL_GROUP] = y_g.astype(BF16)
    ubuf_ref[0:HALO, :] = ubuf_ref[tile:tile + HALO, :]

    o_ref[...] = x + _dot(mix_ref[...], wo_ref[...])


def _odd_mix(x2, g, win, cw, pw, ps, wo, batch, seq, tile):
    nt = seq // tile
    const = lambda b, t: (0, 0)
    row = lambda b, t: (b * nt + t, 0)
    n_in = 3 * CONV_CH + POOL_CH
    return pl.pallas_call(
        functools.partial(_odd_kernel, tile=tile),
        grid=(batch, nt),
        in_specs=[
            pl.BlockSpec((tile, D_MODEL), row),
            pl.BlockSpec((1, D_MODEL), const),
            pl.BlockSpec((D_MODEL, n_in), const),
            pl.BlockSpec((CONV_K, CONV_CH), const),
            pl.BlockSpec((len(POOL_WINDOWS), POOL_GROUP, POOL_GROUP), lambda b, t: (0, 0, 0)),
            pl.BlockSpec((1, POOL_CH), const),
            pl.BlockSpec((D_MODEL, D_MODEL), const),
        ],
        out_specs=pl.BlockSpec((tile, D_MODEL), row),
        out_shape=jax.ShapeDtypeStruct((batch * seq, D_MODEL), F32),
        scratch_shapes=[
            pltpu.VMEM((tile + HALO, CONV_CH), F32),
            pltpu.VMEM((tile + HALO, POOL_CH), F32),
            pltpu.VMEM((tile, CONV_CH + POOL_CH), BF16),
        ],
        compiler_params=pltpu.CompilerParams(
            dimension_semantics=("parallel", "arbitrary"), vmem_limit_bytes=VMEM_LIMIT),
        name="odd_mix",
    )(x2, g, win, cw, pw, ps, wo)


def _gla_tri(tile):
    r = np.arange(tile)
    same = (r[:, None] // GLA_CHUNK) == (r[None, :] // GLA_CHUNK)
    lower = same & (r[None, :] <= r[:, None])
    return jnp.asarray(np.concatenate([lower, same], axis=0).astype(np.float32), dtype=BF16)


def _moba_kaux(seq):
    pos = np.arange(seq)
    nb = seq // MOBA_BLOCK
    aux = np.zeros((seq, LANES), np.float32)
    aux[pos, pos // MOBA_BLOCK] = 1.0
    aux[:, nb] = 1.0
    aux[:, nb + 1] = pos % MOBA_BLOCK
    return jnp.asarray(aux, dtype=BF16)


def _head_group_matrix():
    r = np.arange(MOBA_W)
    same = (r[:, None] // MOBA_DH) == (r[None, :] // MOBA_DH)
    return jnp.asarray(same.astype(np.float32) / MOBA_DH, dtype=BF16)


def _even_weight(w_in):
    qa, ka, va, ra, glr, qb, kb, vb = jnp.split(
        w_in, np.cumsum([GLA_QK, GLA_QK, GLA_V, GLA_V, GLA_GATE_RANK, MOBA_W, MOBA_W])[:].tolist(), axis=-1)
    glr_pad = jnp.pad(glr, ((0, 0), (0, LANES - GLA_GATE_RANK)))
    return jnp.concatenate([qa, ka, ra, glr_pad, va, qb, kb, vb], axis=-1).astype(BF16)


def kernel(x, norm_mix, norm_mlp, w_o, w1, w2, even_w_in, gla_w_gate2, gla_b_gate, gla_out_norm,
           moba_q_norm, moba_k_norm, odd_w_in, conv_w, pool_w, pool_scale):
    batch, seq, d = x.shape
    assert d == D_MODEL and seq % MOBA_BLOCK == 0
    assert seq // MOBA_BLOCK + 2 <= LANES and seq // MOBA_BLOCK % 8 == 0
    depth = norm_mix.shape[0]
    m = batch * seq
    tm = 512
    gla_tile = 256
    odd_tile = 512

    tri = _gla_tri(gla_tile)
    kaux = _moba_kaux(seq)
    eye = jnp.eye(MOBA_BLOCK, dtype=BF16)
    grp = _head_group_matrix()
    slopes = jnp.exp2(-8.0 * jnp.arange(1, MOBA_HEADS + 1, dtype=F32) / MOBA_HEADS)

    x2 = x.reshape(m, d)
    for i in range(depth):
        j = i // 2
        g_mix = norm_mix[i].reshape(1, d)
        wo = w_o[i].astype(BF16)
        if i % 2 == 0:
            w_in = _even_weight(even_w_in[j])
            gq = (jnp.tile(moba_q_norm[j], MOBA_HEADS) * (MOBA_DH ** -0.5)).reshape(1, MOBA_W)
            gk = jnp.tile(moba_k_norm[j], MOBA_HEADS).reshape(1, MOBA_W)
            gla_in, va, qb, kb, vb = _even_inproj(x2, g_mix, w_in, gq, gk, grp, tm)
            wg = jnp.pad(gla_w_gate2[j], ((0, LANES - GLA_GATE_RANK), (0, 0))).astype(BF16)
            oa = _gla(gla_in, va, wg, gla_b_gate[j].reshape(1, GLA_QK),
                      gla_out_norm[j].reshape(1, GLA_DV), tri, batch, seq, gla_tile)
            ob = _moba(slopes, qb, kb, vb, kaux, eye, batch, seq)
            x2 = _outproj(x2, oa, ob, wo, tm)
        else:
            x2 = _odd_mix(x2, g_mix, odd_w_in[j].astype(BF16), conv_w[j], pool_w[j].astype(BF16),
                          pool_scale[j].reshape(1, POOL_CH), wo, batch, seq, odd_tile)
        x2 = _mlp(x2, norm_mlp[i].reshape(1, d), w1[i].astype(BF16), w2[i].astype(BF16), 1024, 512)
    return x2.reshape(batch, seq, d)
```

```python
import functools

import numpy as np
import jax
import jax.numpy as jnp
from jax import lax
from jax.experimental import pallas as pl
from jax.experimental.pallas import tpu as pltpu

F32 = jnp.float32
BF16 = jnp.bfloat16

D_MODEL = 1024
GLA_HEADS = 4
GLA_DK = 64
GLA_DV = 128
GLA_GATE_RANK = 16
GLA_GATE_TEMP = 16.0
GLA_CHUNK = 64
MOBA_HEADS = 8
MOBA_DH = 64
MOBA_BLOCK = 256
MOBA_TOPK = 3
CONV_CH = 512
CONV_K = 3
POOL_CH = 512
POOL_WINDOWS = (2, 4, 8, 16)
POOL_GROUP = 128
D_FF = 4096
EPS = 1e-6

GLA_QK = GLA_HEADS * GLA_DK
GLA_V = GLA_HEADS * GLA_DV
MOBA_W = MOBA_HEADS * MOBA_DH
LANES = 128
HALO = 16
NEG = -1e30

GLA_F32_W = GLA_QK + GLA_QK + GLA_V + LANES
EVEN_N = GLA_F32_W + GLA_V + 3 * MOBA_W

VMEM_LIMIT = 48 * 1024 * 1024
POST_VMEM_LIMIT = 56 * 1024 * 1024

NT_DIMS = (((1,), (1,)), ((), ()))
TN_DIMS = (((0,), (0,)), ((), ()))


def _dot(a, b):
    return jnp.dot(a, b, preferred_element_type=F32)


def _dot_nt(a, b):
    return lax.dot_general(a, b, NT_DIMS, preferred_element_type=F32)


def _dot_tn(a, b):
    return lax.dot_general(a, b, TN_DIMS, preferred_element_type=F32)


def _rmsnorm_rows(x, g):
    ms = jnp.mean(x * x, axis=-1, keepdims=True)
    return x * lax.rsqrt(ms + EPS) * g


def _split2(y):
    hi = y.astype(BF16)
    lo = (y - hi.astype(F32)).astype(BF16)
    return hi, lo


def _even_inproj_kernel(x_ref, g_ref, w_ref, gq_ref, gk_ref, grp_ref,
                        gla_ref, va_ref, qb_ref, kb_ref, vb_ref):
    h = _rmsnorm_rows(x_ref[...], g_ref[...]).astype(BF16)
    c0 = GLA_F32_W
    c1 = c0 + GLA_V
    c2 = c1 + MOBA_W
    c3 = c2 + MOBA_W
    gla_ref[...] = _dot(h, w_ref[:, 0:c0])
    va_ref[...] = _dot(h, w_ref[:, c0:c1]).astype(BF16)
    vb_ref[...] = _dot(h, w_ref[:, c3:c3 + MOBA_W]).astype(BF16)

    def headnorm(y, gain):
        ms = _dot((y * y).astype(BF16), grp_ref[...])
        return (y * lax.rsqrt(ms + EPS) * gain).astype(BF16)

    qb_ref[...] = headnorm(_dot(h, w_ref[:, c1:c2]), gq_ref[...])
    kb_ref[...] = headnorm(_dot(h, w_ref[:, c2:c3]), gk_ref[...])


def _even_inproj(x2, g, w, gq, gk, grp, tm):
    m = x2.shape[0]
    const = lambda i: (0, 0)
    row = lambda i: (i, 0)
    return pl.pallas_call(
        _even_inproj_kernel,
        grid=(m // tm,),
        in_specs=[
            pl.BlockSpec((tm, D_MODEL), row),
            pl.BlockSpec((1, D_MODEL), const),
            pl.BlockSpec((D_MODEL, EVEN_N), const),
            pl.BlockSpec((1, MOBA_W), const),
            pl.BlockSpec((1, MOBA_W), const),
            pl.BlockSpec((MOBA_W, MOBA_W), const),
        ],
        out_specs=[
            pl.BlockSpec((tm, GLA_F32_W), row),
            pl.BlockSpec((tm, GLA_V), row),
            pl.BlockSpec((tm, MOBA_W), row),
            pl.BlockSpec((tm, MOBA_W), row),
            pl.BlockSpec((tm, MOBA_W), row),
        ],
        out_shape=[
            jax.ShapeDtypeStruct((m, GLA_F32_W), F32),
            jax.ShapeDtypeStruct((m, GLA_V), BF16),
            jax.ShapeDtypeStruct((m, MOBA_W), BF16),
            jax.ShapeDtypeStruct((m, MOBA_W), BF16),
            jax.ShapeDtypeStruct((m, MOBA_W), BF16),
        ],
        compiler_params=pltpu.CompilerParams(
            dimension_semantics=("parallel",), vmem_limit_bytes=VMEM_LIMIT),
        name="even_inproj",
    )(x2, g, w, gq, gk, grp)


def _gla_kernel(gla_ref, va_ref, wg_ref, bg_ref, gn_ref, tri_ref, o_ref, state_ref, *, tile):
    n_chunks = tile // GLA_CHUNK

    @pl.when(pl.program_id(1) == 0)
    def _():
        state_ref[...] = jnp.zeros_like(state_ref)

    r0 = 2 * GLA_QK
    glr = gla_ref[:, r0 + GLA_V:r0 + GLA_V + LANES].astype(BF16)
    pre = _dot(glr, wg_ref[...]) + bg_ref[...]
    log_a = (jnp.minimum(pre, 0.0) - jnp.log1p(jnp.exp(-jnp.abs(pre)))) * (1.0 / GLA_GATE_TEMP)

    g1 = log_a.astype(BF16)
    rem = log_a - g1.astype(F32)
    g2 = rem.astype(BF16)
    g3 = (rem - g2.astype(F32)).astype(BF16)
    tri = tri_ref[...]
    bb = _dot(tri, g1) + _dot(tri, g2) + _dot(tri, g3)
    b = bb[0:tile]
    btot = bb[tile:2 * tile]

    q_ = gla_ref[:, 0:GLA_QK] * (GLA_DK ** -0.5) * jnp.exp(b)
    k = gla_ref[:, GLA_QK:2 * GLA_QK]
    k_ = k * jnp.exp(-b)
    k_end = k * jnp.exp(btot - b)
    causal = tri[0:tile] > 0
    lane = lax.broadcasted_iota(jnp.int32, (tile, LANES), 1)

    for pair in range(GLA_HEADS // 2):
        ls = slice(pair * LANES, (pair + 1) * LANES)
        qp = q_[:, ls]
        kp = k_[:, ls].astype(BF16)
        kep = k_end[:, ls]
        btp = btot[:, ls]
        for sub in range(2):
            head = 2 * pair + sub
            hs = slice(head * GLA_DV, (head + 1) * GLA_DV)
            in_head = (lane < GLA_DK) if sub == 0 else (lane >= GLA_DK)
            qm = jnp.where(in_head, qp, 0.0).astype(BF16)
            kem = jnp.where(in_head, kep, 0.0).astype(BF16)
            v_h = va_ref[:, hs]
            att = jnp.where(causal, _dot_nt(qm, kp), 0.0).astype(BF16)
            o_intra = _dot(att, v_h)
            st = state_ref[head]
            inter = []
            for c in range(n_chunks):
                rows = slice(c * GLA_CHUNK, (c + 1) * GLA_CHUNK)
                inter.append(_dot_nt(qm[rows], st.astype(BF16)))
                decay = jnp.exp(btp[c * GLA_CHUNK:c * GLA_CHUNK + 1, :])
                st = st * decay + _dot_tn(v_h[rows], kem[rows])
            state_ref[head] = st
            o = o_intra + jnp.concatenate(inter, axis=0)
            on = _rmsnorm_rows(o, gn_ref[...])
            r = gla_ref[:, r0 + head * GLA_DV:r0 + (head + 1) * GLA_DV]
            o_ref[:, hs] = (on * (r * jax.nn.sigmoid(r))).astype(BF16)


def _gla(gla_in, va, wg, bg, gn, tri, batch, seq, tile):
    nt = seq // tile
    const = lambda b, t: (0, 0)
    row = lambda b, t: (b * nt + t, 0)
    return pl.pallas_call(
        functools.partial(_gla_kernel, tile=tile),
        grid=(batch, nt),
        in_specs=[
            pl.BlockSpec((tile, GLA_F32_W), row),
            pl.BlockSpec((tile, GLA_V), row),
            pl.BlockSpec((LANES, GLA_QK), const),
            pl.BlockSpec((1, GLA_QK), const),
            pl.BlockSpec((1, GLA_DV), const),
            pl.BlockSpec((2 * tile, tile), const),
        ],
        out_specs=pl.BlockSpec((tile, GLA_V), row),
        out_shape=jax.ShapeDtypeStruct((batch * seq, GLA_V), BF16),
        scratch_shapes=[pltpu.VMEM((GLA_HEADS, GLA_DV, LANES), F32)],
        compiler_params=pltpu.CompilerParams(
            dimension_semantics=("parallel", "arbitrary"), vmem_limit_bytes=VMEM_LIMIT),
        name="gla",
    )(gla_in, va, wg, bg, gn, tri)


def _moba_kernel(slopes_ref, q_ref, k_ref, v_ref, kaux_ref, o_ref,
                 kaug_ref, vt_ref, qt_ref, qaug_ref, km_ref, ot_ref, *, n_blocks):
    bs = MOBA_BLOCK
    nb = n_blocks
    pair_keys = 2 * bs
    n_aux = LANES - nb
    hp = pl.program_id(1)
    lane1 = lax.broadcasted_iota(jnp.int32, (1, LANES), 1)

    for jj in range(nb // 2):
        rows = slice(jj * pair_keys, (jj + 1) * pair_keys)
        kaug_ref[jj, :, 0:LANES] = k_ref[rows, :]
        kaug_ref[jj, :, LANES:2 * LANES] = kaux_ref[rows, :]
        vt_ref[jj] = jnp.transpose(v_ref[rows, :])
    for j in range(nb):
        rows = slice(j * bs, (j + 1) * bs)
        qt_ref[j] = jnp.transpose(q_ref[rows, :])
        km = jnp.sum(k_ref[rows, :].astype(F32), axis=0, keepdims=True) * (1.0 / bs)
        km_ref[j:j + 1, :] = jnp.where(lane1 < MOBA_DH, km, 0.0)
        km_ref[nb + j:nb + j + 1, :] = jnp.where(lane1 >= MOBA_DH, km, 0.0)
    km_hi, km_lo = _split2(km_ref[...])
    aux_row = lax.broadcasted_iota(jnp.int32, (n_aux, bs), 0)
    t_lo = lax.broadcasted_iota(jnp.int32, (n_aux, bs), 1).astype(F32)
    for sub in range(2):
        slope = slopes_ref[2 * hp + sub]
        aux = jnp.where(aux_row == 0, -slope * t_lo, jnp.where(aux_row == 1, slope, 0.0)).astype(BF16)
        for j in range(nb):
            qaug_ref[sub, j, LANES + nb:2 * LANES, :] = aux

    jrow = lax.broadcasted_iota(jnp.int32, (nb, bs), 0)
    drow = lax.broadcasted_iota(jnp.int32, (LANES, bs), 0)

    def prepare_queries(i):
        qt = qt_ref[i]
        gate_both = _dot(km_hi, qt) + _dot(km_lo, qt)
        for sub in range(2):
            slope = slopes_ref[2 * hp + sub]
            gate = jnp.where(jrow < i, gate_both[sub * nb:(sub + 1) * nb], -jnp.inf)
            rank = jnp.zeros((nb, bs), jnp.int32)
            for jp in range(nb):
                g_jp = gate[jp:jp + 1, :]
                tie = jnp.where(g_jp == gate, jnp.where(jrow > jp, 1, 0), 0)
                rank = rank + jnp.where(g_jp > gate, 1, tie)
            rank = jnp.where(jrow < i, rank, jnp.where(jrow == i, 0, MOBA_TOPK))
            dist_blocks = (i - jrow).astype(F32) * float(bs)
            bias = jnp.where(rank < MOBA_TOPK, -slope * dist_blocks, NEG).astype(BF16)
            in_head = (drow < MOBA_DH) if sub == 0 else (drow >= MOBA_DH)
            qaug_ref[sub, i, 0:LANES, :] = jnp.where(in_head, qt, jnp.zeros_like(qt))
            qaug_ref[sub, i, LANES:LANES + nb, :] = bias

    prepare_queries(0)

    key_minus_query = (lax.broadcasted_iota(jnp.int32, (pair_keys, bs), 0)
                       - lax.broadcasted_iota(jnp.int32, (pair_keys, bs), 1))

    def attend(i, carry):
        prepare_queries(jnp.minimum(i + 1, nb - 1))
        qa = [qaug_ref[sub, i] for sub in range(2)]
        own = i // 2

        def scores(jj):
            kj = kaug_ref[jj]
            return [_dot(kj, qa[sub]) for sub in range(2)]

        def update(st, ss, jj):
            vj = vt_ref[jj]
            ps, ms, ls, als = [], [], [], []
            for sub in range(2):
                m, l, _ = st[3 * sub:3 * sub + 3]
                s = ss[sub]
                m_new = jnp.maximum(m, jnp.max(s, axis=0, keepdims=True))
                alpha = jnp.exp(m - m_new)
                p = jnp.exp(s - m_new)
                ls.append(alpha * l + jnp.sum(p, axis=0, keepdims=True))
                ps.append(p.astype(BF16))
                ms.append(m_new)
                als.append(alpha)
            pvs = [_dot(vj, ps[sub]) for sub in range(2)]
            new = []
            for sub in range(2):
                new += [ms[sub], ls[sub], als[sub] * st[3 * sub + 2] + pvs[sub]]
            return tuple(new)

        limit = (i - 2 * own) * bs
        other = jnp.where(own % 2 == 1, own - 1, jnp.minimum(own + 1, nb // 2 - 1))
        ss = [jnp.where(key_minus_query <= limit, s, NEG) for s in scores(own)]
        ss_o = scores(other)
        vj = vt_ref[own]
        ms = [jnp.max(s, axis=0, keepdims=True) for s in ss]
        ps = [jnp.exp(s - m) for s, m in zip(ss, ms)]
        state = []
        for sub in range(2):
            state += [ms[sub], jnp.sum(ps[sub], axis=0, keepdims=True), _dot(vj, ps[sub].astype(BF16))]
        state = update(tuple(state), ss_o, other)

        def past_group(t, st):
            ss_a = scores(2 * t)
            ss_b = scores(2 * t + 1)
            st = update(st, ss_a, 2 * t)
            return update(st, ss_b, 2 * t + 1)

        st = lax.fori_loop(0, own // 2, past_group, state)
        ot_ref[i] = jnp.where(drow < MOBA_DH, st[2] / st[1], st[5] / st[4]).astype(BF16)
        return carry

    lax.fori_loop(0, nb, attend, 0)
    for j in range(nb):
        o_ref[j * bs:(j + 1) * bs, :] = jnp.transpose(ot_ref[j])


def _moba(slopes, q, k, v, kaux, batch, seq):
    nb = seq // MOBA_BLOCK
    n_pairs = MOBA_HEADS // 2
    blk = lambda b, h, s: (b, h)
    return pl.pallas_call(
        functools.partial(_moba_kernel, n_blocks=nb),
        grid_spec=pltpu.PrefetchScalarGridSpec(
            num_scalar_prefetch=1,
            grid=(batch, n_pairs),
            in_specs=[
                pl.BlockSpec((seq, LANES), blk),
                pl.BlockSpec((seq, LANES), blk),
                pl.BlockSpec((seq, LANES), blk),
                pl.BlockSpec((seq, LANES), lambda b, h, s: (0, 0)),
            ],
            out_specs=pl.BlockSpec((seq, LANES), blk),
            scratch_shapes=[
                pltpu.VMEM((nb // 2, 2 * MOBA_BLOCK, 2 * LANES), BF16),
                pltpu.VMEM((nb // 2, LANES, 2 * MOBA_BLOCK), BF16),
                pltpu.VMEM((nb, LANES, MOBA_BLOCK), BF16),
                pltpu.VMEM((2, nb, 2 * LANES, MOBA_BLOCK), BF16),
                pltpu.VMEM((2 * nb, LANES), F32),
                pltpu.VMEM((nb, LANES, MOBA_BLOCK), BF16),
            ],
        ),
        out_shape=jax.ShapeDtypeStruct((batch * seq, MOBA_W), BF16),
        compiler_params=pltpu.CompilerParams(
            dimension_semantics=("parallel", "parallel"), vmem_limit_bytes=VMEM_LIMIT),
        name="moba",
    )(slopes, q, k, v, kaux)


def _post_kernel(*refs, n_parts, tf):
    x_ref = refs[0]
    part_refs = refs[1:1 + n_parts]
    wo_ref, g_ref, w1_ref, w2_ref, o_ref, a_ref = refs[1 + n_parts:]
    x = x_ref[...]
    row = 0
    for p_ref in part_refs:
        width = p_ref.shape[1]
        x = x + _dot(p_ref[...], wo_ref[row:row + width, :])
        row += width
    h = _rmsnorm_rows(x, g_ref[...]).astype(BF16)
    for c in range(D_FF // tf):
        cols = slice(c * tf, (c + 1) * tf)
        a = jnp.maximum(_dot(h, w1_ref[:, cols]), 0.0)
        a_ref[:, cols] = (a * a).astype(BF16)
    o_ref[...] = x + _dot(a_ref[...], w2_ref[...])


def _post(x2, parts, wo, g, w1, w2, tm, tf):
    m = x2.shape[0]
    const = lambda i: (0, 0)
    row = lambda i: (i, 0)
    resident = dict(pipeline_mode=pl.Buffered(1))
    return pl.pallas_call(
        functools.partial(_post_kernel, n_parts=len(parts), tf=tf),
        grid=(m // tm,),
        in_specs=(
            [pl.BlockSpec((tm, D_MODEL), row)]
            + [pl.BlockSpec((tm, p.shape[1]), row) for p in parts]
            + [pl.BlockSpec((D_MODEL, D_MODEL), const, **resident),
               pl.BlockSpec((1, D_MODEL), const),
               pl.BlockSpec((D_MODEL, D_FF), const, **resident),
               pl.BlockSpec((D_FF, D_MODEL), const, **resident)]),
        out_specs=pl.BlockSpec((tm, D_MODEL), row),
        out_shape=jax.ShapeDtypeStruct((m, D_MODEL), F32),
        scratch_shapes=[pltpu.VMEM((tm, D_FF), BF16)],
        compiler_params=pltpu.CompilerParams(
            dimension_semantics=("parallel",), vmem_limit_bytes=POST_VMEM_LIMIT),
        name="post",
    )(x2, *parts, wo, g, w1, w2)


def _odd_kernel(x_ref, g_ref, win_ref, cw_ref, pw_ref, ps_ref, mix_ref, zbuf_ref, ubuf_ref, *, tile):
    t_idx = pl.program_id(1)

    @pl.when(t_idx == 0)
    def _():
        zbuf_ref[0:HALO, :] = jnp.zeros((HALO, CONV_CH), F32)
        ubuf_ref[0:HALO, :] = jnp.zeros((HALO, POOL_CH), F32)

    h = _rmsnorm_rows(x_ref[...], g_ref[...]).astype(BF16)
    bg = _dot(h, win_ref[:, 0:CONV_CH])
    cg = _dot(h, win_ref[:, CONV_CH:2 * CONV_CH])
    xc = _dot(h, win_ref[:, 2 * CONV_CH:3 * CONV_CH])
    u = _dot(h, win_ref[:, 3 * CONV_CH:3 * CONV_CH + POOL_CH])

    z = cg * xc
    zbuf_ref[HALO:HALO + tile, :] = z
    y = z * cw_ref[CONV_K - 1:CONV_K, :]
    for tap in range(CONV_K - 1):
        shift = CONV_K - 1 - tap
        y = y + zbuf_ref[HALO - shift:HALO - shift + tile, :] * cw_ref[tap:tap + 1, :]
    mix_ref[:, 0:CONV_CH] = (bg * y).astype(BF16)
    zbuf_ref[0:HALO, :] = zbuf_ref[tile:tile + HALO, :]

    ubuf_ref[HALO:HALO + tile, :] = u
    pos = lax.broadcasted_iota(jnp.int32, (tile, POOL_GROUP), 0) + t_idx * tile
    for grp, win in enumerate(POOL_WINDOWS):
        cs = slice(grp * POOL_GROUP, (grp + 1) * POOL_GROUP)
        u_g = u[:, cs]
        tot = u_g
        for shift in range(1, win):
            tot = tot + ubuf_ref[HALO - shift:HALO - shift + tile, cs]
        cnt = jnp.minimum(pos + 1, win).astype(F32)
        pooled = (tot / cnt - u_g).astype(BF16)
        y_g = _dot(pooled, pw_ref[grp]) * ps_ref[:, cs]
        mix_ref[:, CONV_CH + grp * POOL_GROUP:CONV_CH + (grp + 1) * POOL_GROUP] = y_g.astype(BF16)
    ubuf_ref[0:HALO, :] = ubuf_ref[tile:tile + HALO, :]


def _odd_mix(x2, g, win, cw, pw, ps, batch, seq, tile):
    nt = seq // tile
    const = lambda b, t: (0, 0)
    row = lambda b, t: (b * nt + t, 0)
    n_in = 3 * CONV_CH + POOL_CH
    return pl.pallas_call(
        functools.partial(_odd_kernel, tile=tile),
        grid=(batch, nt),
        in_specs=[
            pl.BlockSpec((tile, D_MODEL), row),
            pl.BlockSpec((1, D_MODEL), const),
            pl.BlockSpec((D_MODEL, n_in), const),
            pl.BlockSpec((CONV_K, CONV_CH), const),
            pl.BlockSpec((len(POOL_WINDOWS), POOL_GROUP, POOL_GROUP), lambda b, t: (0, 0, 0)),
            pl.BlockSpec((1, POOL_CH), const),
        ],
        out_specs=pl.BlockSpec((tile, CONV_CH + POOL_CH), row),
        out_shape=jax.ShapeDtypeStruct((batch * seq, CONV_CH + POOL_CH), BF16),
        scratch_shapes=[
            pltpu.VMEM((tile + HALO, CONV_CH), F32),
            pltpu.VMEM((tile + HALO, POOL_CH), F32),
        ],
        compiler_params=pltpu.CompilerParams(
            dimension_semantics=("parallel", "arbitrary"), vmem_limit_bytes=VMEM_LIMIT),
        name="odd_mix",
    )(x2, g, win, cw, pw, ps)


def _gla_tri(tile):
    r = np.arange(tile)
    same = (r[:, None] // GLA_CHUNK) == (r[None, :] // GLA_CHUNK)
    lower = same & (r[None, :] <= r[:, None])
    return jnp.asarray(np.concatenate([lower, same], axis=0).astype(np.float32), dtype=BF16)


def _moba_kaux(seq):
    pos = np.arange(seq)
    nb = seq // MOBA_BLOCK
    aux = np.zeros((seq, LANES), np.float32)
    aux[pos, pos // MOBA_BLOCK] = 1.0
    aux[:, nb] = 1.0
    aux[:, nb + 1] = pos % MOBA_BLOCK
    return jnp.asarray(aux, dtype=BF16)


def _head_group_matrix():
    r = np.arange(MOBA_W)
    same = (r[:, None] // MOBA_DH) == (r[None, :] // MOBA_DH)
    return jnp.asarray(same.astype(np.float32) / MOBA_DH, dtype=BF16)


def _even_weight(w_in):
    qa, ka, va, ra, glr, qb, kb, vb = jnp.split(
        w_in, np.cumsum([GLA_QK, GLA_QK, GLA_V, GLA_V, GLA_GATE_RANK, MOBA_W, MOBA_W])[:].tolist(), axis=-1)
    glr_pad = jnp.pad(glr, ((0, 0), (0, LANES - GLA_GATE_RANK)))
    return jnp.concatenate([qa, ka, ra, glr_pad, va, qb, kb, vb], axis=-1).astype(BF16)


def kernel(x, norm_mix, norm_mlp, w_o, w1, w2, even_w_in, gla_w_gate2, gla_b_gate, gla_out_norm,
           moba_q_norm, moba_k_norm, odd_w_in, conv_w, pool_w, pool_scale):
    batch, seq, d = x.shape
    assert d == D_MODEL and seq % (4 * MOBA_BLOCK) == 0
    assert seq // MOBA_BLOCK + 2 <= LANES and seq // MOBA_BLOCK % 16 == 0
    depth = norm_mix.shape[0]
    m = batch * seq
    tm = 512
    gla_tile = 256
    odd_tile = 512
    post_tm = 1024
    post_tf = 512

    tri = _gla_tri(gla_tile)
    kaux = _moba_kaux(seq)
    grp = _head_group_matrix()
    slopes = jnp.exp2(-8.0 * jnp.arange(1, MOBA_HEADS + 1, dtype=F32) / MOBA_HEADS)

    x2 = x.reshape(m, d)
    for i in range(depth):
        j = i // 2
        g_mix = norm_mix[i].reshape(1, d)
        wo = w_o[i].astype(BF16)
        if i % 2 == 0:
            w_in = _even_weight(even_w_in[j])
            gq = (jnp.tile(moba_q_norm[j], MOBA_HEADS) * (MOBA_DH ** -0.5)).reshape(1, MOBA_W)
            gk = jnp.tile(moba_k_norm[j], MOBA_HEADS).reshape(1, MOBA_W)
            gla_in, va, qb, kb, vb = _even_inproj(x2, g_mix, w_in, gq, gk, grp, tm)
            wg = jnp.pad(gla_w_gate2[j], ((0, LANES - GLA_GATE_RANK), (0, 0))).astype(BF16)
            oa = _gla(gla_in, va, wg, gla_b_gate[j].reshape(1, GLA_QK),
                      gla_out_norm[j].reshape(1, GLA_DV), tri, batch, seq, gla_tile)
            ob = _moba(slopes, qb, kb, vb, kaux, batch, seq)
            parts = [oa, ob]
        else:
            parts = [_odd_mix(x2, g_mix, odd_w_in[j].astype(BF16), conv_w[j], pool_w[j].astype(BF16),
                              pool_scale[j].reshape(1, POOL_CH), batch, seq, odd_tile)]
        x2 = _post(x2, parts, wo, norm_mlp[i].reshape(1, d), w1[i].astype(BF16), w2[i].astype(BF16),
                   post_tm, post_tf)
    return x2.reshape(batch, seq, d)
```

```python
import functools

import numpy as np
import jax
import jax.numpy as jnp
from jax import lax
from jax.experimental import pallas as pl
from jax.experimental.pallas import tpu as pltpu

F32 = jnp.float32
BF16 = jnp.bfloat16

D_MODEL = 1024
GLA_HEADS = 4
GLA_DK = 64
GLA_DV = 128
GLA_GATE_RANK = 16
GLA_GATE_TEMP = 16.0
GLA_CHUNK = 64
MOBA_HEADS = 8
MOBA_DH = 64
MOBA_BLOCK = 256
MOBA_TOPK = 3
CONV_CH = 512
CONV_K = 3
POOL_CH = 512
POOL_WINDOWS = (2, 4, 8, 16)
POOL_GROUP = 128
D_FF = 4096
EPS = 1e-6

GLA_QK = GLA_HEADS * GLA_DK
GLA_V = GLA_HEADS * GLA_DV
MOBA_W = MOBA_HEADS * MOBA_DH
LANES = 128
HALO = 16
NEG = -1e30

GLA_F32_W = GLA_QK + GLA_QK + GLA_V + LANES
EVEN_N = GLA_F32_W + GLA_V + 3 * MOBA_W

VMEM_LIMIT = 48 * 1024 * 1024
POST_VMEM_LIMIT = 56 * 1024 * 1024

NT_DIMS = (((1,), (1,)), ((), ()))
TN_DIMS = (((0,), (0,)), ((), ()))


def _dot(a, b):
    return jnp.dot(a, b, preferred_element_type=F32)


def _dot_nt(a, b):
    return lax.dot_general(a, b, NT_DIMS, preferred_element_type=F32)


def _dot_tn(a, b):
    return lax.dot_general(a, b, TN_DIMS, preferred_element_type=F32)


def _rmsnorm_rows(x, g):
    ms = jnp.mean(x * x, axis=-1, keepdims=True)
    return x * lax.rsqrt(ms + EPS) * g


def _split2(y):
    hi = y.astype(BF16)
    lo = (y - hi.astype(F32)).astype(BF16)
    return hi, lo


def _even_inproj_kernel(x_ref, g_ref, w_ref, gq_ref, gk_ref, grp_ref,
                        gla_ref, va_ref, qb_ref, kb_ref, vb_ref, *, n_sub):
    c0 = GLA_F32_W
    c1 = c0 + GLA_V
    c2 = c1 + MOBA_W
    c3 = c2 + MOBA_W
    sub_rows = x_ref.shape[0] // n_sub

    def headnorm(y, gain):
        ms = _dot((y * y).astype(BF16), grp_ref[...])
        return (y * lax.rsqrt(ms + EPS) * gain).astype(BF16)

    for s in range(n_sub):
        rows = slice(s * sub_rows, (s + 1) * sub_rows)
        h = _rmsnorm_rows(x_ref[rows, :], g_ref[...]).astype(BF16)
        qb = _dot(h, w_ref[:, c1:c2])
        kb = _dot(h, w_ref[:, c2:c3])
        gla_ref[rows, :] = _dot(h, w_ref[:, 0:c0])
        va_ref[rows, :] = _dot(h, w_ref[:, c0:c1]).astype(BF16)
        vb_ref[rows, :] = _dot(h, w_ref[:, c3:c3 + MOBA_W]).astype(BF16)
        qb_ref[rows, :] = headnorm(qb, gq_ref[...])
        kb_ref[rows, :] = headnorm(kb, gk_ref[...])


def _even_inproj(x2, g, w, gq, gk, grp, tm):
    m = x2.shape[0]
    const = lambda i: (0, 0)
    row = lambda i: (i, 0)
    return pl.pallas_call(
        functools.partial(_even_inproj_kernel, n_sub=2),
        grid=(m // tm,),
        in_specs=[
            pl.BlockSpec((tm, D_MODEL), row),
            pl.BlockSpec((1, D_MODEL), const),
            pl.BlockSpec((D_MODEL, EVEN_N), const, pipeline_mode=pl.Buffered(1)),
            pl.BlockSpec((1, MOBA_W), const),
            pl.BlockSpec((1, MOBA_W), const),
            pl.BlockSpec((MOBA_W, MOBA_W), const),
        ],
        out_specs=[
            pl.BlockSpec((tm, GLA_F32_W), row),
            pl.BlockSpec((tm, GLA_V), row),
            pl.BlockSpec((tm, MOBA_W), row),
            pl.BlockSpec((tm, MOBA_W), row),
            pl.BlockSpec((tm, MOBA_W), row),
        ],
        out_shape=[
            jax.ShapeDtypeStruct((m, GLA_F32_W), F32),
            jax.ShapeDtypeStruct((m, GLA_V), BF16),
            jax.ShapeDtypeStruct((m, MOBA_W), BF16),
            jax.ShapeDtypeStruct((m, MOBA_W), BF16),
            jax.ShapeDtypeStruct((m, MOBA_W), BF16),
        ],
        compiler_params=pltpu.CompilerParams(
            dimension_semantics=("parallel",), vmem_limit_bytes=VMEM_LIMIT),
        name="even_inproj",
    )(x2, g, w, gq, gk, grp)


def _gla_kernel(gla_ref, va_ref, wg_ref, bg_ref, gn_ref, tri_ref, o_ref, state_ref, *, sub_tile):
    n_sub = gla_ref.shape[0] // sub_tile
    n_chunks = sub_tile // GLA_CHUNK
    heads = range(GLA_HEADS)
    chunks = [slice(c * GLA_CHUNK, (c + 1) * GLA_CHUNK) for c in range(n_chunks)]
    r0 = 2 * GLA_QK
    tri = tri_ref[...]
    causal = tri > 0
    lane = lax.broadcasted_iota(jnp.int32, (sub_tile, LANES), 1)

    @pl.when(pl.program_id(1) == 0)
    def _():
        state_ref[...] = jnp.zeros_like(state_ref)

    def pair_slice(head):
        return slice((head // 2) * LANES, (head // 2 + 1) * LANES)

    def in_head(head):
        return (lane < GLA_DK) if head % 2 == 0 else (lane >= GLA_DK)

    pre = []
    for s in range(n_sub):
        rows = slice(s * sub_tile, (s + 1) * sub_tile)
        glr = gla_ref[rows, r0 + GLA_V:r0 + GLA_V + LANES].astype(BF16)
        z = _dot(glr, wg_ref[...]) + bg_ref[...]
        log_a = (jnp.minimum(z, 0.0) - jnp.log1p(jnp.exp(-jnp.abs(z)))) * (1.0 / GLA_GATE_TEMP)
        g_hi, g_lo = _split2(log_a)
        bb = _dot(tri, jnp.concatenate([g_hi, g_lo], axis=1))
        b = bb[:, 0:GLA_QK] + bb[:, GLA_QK:2 * GLA_QK]
        b_last = [b[(c + 1) * GLA_CHUNK - 1:(c + 1) * GLA_CHUNK, :] for c in range(n_chunks)]
        btot = jnp.concatenate([jnp.broadcast_to(bl, (GLA_CHUNK, GLA_QK)) for bl in b_last], axis=0)
        q_ = gla_ref[rows, 0:GLA_QK] * (GLA_DK ** -0.5) * jnp.exp(b)
        k = gla_ref[rows, GLA_QK:2 * GLA_QK]
        k_ = k * jnp.exp(-b)
        k_end = k * jnp.exp(btot - b)
        qm = [jnp.where(in_head(h), q_[:, pair_slice(h)], 0.0).astype(BF16) for h in heads]
        kem = [jnp.where(in_head(h), k_end[:, pair_slice(h)], 0.0).astype(BF16) for h in heads]
        v = [va_ref[rows, h * GLA_DV:(h + 1) * GLA_DV] for h in heads]
        d_state = [[_dot_tn(v[h][cr], kem[h][cr]) for cr in chunks] for h in heads]
        att = [jnp.where(causal, _dot_nt(qm[h], k_[:, pair_slice(h)].astype(BF16)), 0.0).astype(BF16)
               for h in heads]
        o_intra = [_dot(att[h], v[h]) for h in heads]
        decay = [[jnp.exp(bl[:, pair_slice(h)]) for bl in b_last] for h in heads]
        pre.append((qm, d_state, o_intra, decay))

    states = [[] for _ in range(n_sub)]
    for h in heads:
        st = state_ref[h]
        for s in range(n_sub):
            _, d_state, _, decay = pre[s]
            per_chunk = []
            for c in range(n_chunks):
                per_chunk.append(st.astype(BF16))
                st = st * decay[h][c] + d_state[h][c]
            states[s].append(per_chunk)
        state_ref[h] = st

    for s in range(n_sub):
        rows = slice(s * sub_tile, (s + 1) * sub_tile)
        qm, _, o_intra, _ = pre[s]
        for h in heads:
            o_inter = jnp.concatenate(
                [_dot_nt(qm[h][cr], states[s][h][c]) for c, cr in enumerate(chunks)], axis=0)
            on = _rmsnorm_rows(o_intra[h] + o_inter, gn_ref[...])
            r = gla_ref[rows, r0 + h * GLA_DV:r0 + (h + 1) * GLA_DV]
            o_ref[rows, h * GLA_DV:(h + 1) * GLA_DV] = (on * (r * jax.nn.sigmoid(r))).astype(BF16)


def _gla(gla_in, va, wg, bg, gn, tri, batch, seq, tile, sub_tile):
    nt = seq // tile
    const = lambda b, t: (0, 0)
    row = lambda b, t: (b * nt + t, 0)
    return pl.pallas_call(
        functools.partial(_gla_kernel, sub_tile=sub_tile),
        grid=(batch, nt),
        in_specs=[
            pl.BlockSpec((tile, GLA_F32_W), row),
            pl.BlockSpec((tile, GLA_V), row),
            pl.BlockSpec((LANES, GLA_QK), const),
            pl.BlockSpec((1, GLA_QK), const),
            pl.BlockSpec((1, GLA_DV), const),
            pl.BlockSpec((sub_tile, sub_tile), const),
        ],
        out_specs=pl.BlockSpec((tile, GLA_V), row),
        out_shape=jax.ShapeDtypeStruct((batch * seq, GLA_V), BF16),
        scratch_shapes=[pltpu.VMEM((GLA_HEADS, GLA_DV, LANES), F32)],
        compiler_params=pltpu.CompilerParams(
            dimension_semantics=("parallel", "arbitrary"), vmem_limit_bytes=VMEM_LIMIT),
        name="gla",
    )(gla_in, va, wg, bg, gn, tri)


def _moba_kernel(slopes_ref, q_ref, k_ref, v_ref, kaux_ref, o_ref,
                 kaug_ref, vt_ref, qt_ref, qaug_ref, km_ref, ot_ref, *, n_blocks):
    bs = MOBA_BLOCK
    nb = n_blocks
    n_pairs = nb // 2
    pw = 2 * bs
    n_aux = LANES - nb
    hp = pl.program_id(1)
    lane1 = lax.broadcasted_iota(jnp.int32, (1, LANES), 1)

    for jj in range(n_pairs):
        rows = slice(jj * pw, (jj + 1) * pw)
        kaug_ref[jj, :, 0:LANES] = k_ref[rows, :]
        kaug_ref[jj, :, LANES:2 * LANES] = kaux_ref[rows, :]
        vt_ref[jj] = jnp.transpose(v_ref[rows, :])
        qt_ref[jj] = jnp.transpose(q_ref[rows, :])
    for j in range(nb):
        km = jnp.sum(k_ref[j * bs:(j + 1) * bs, :].astype(F32), axis=0, keepdims=True) * (1.0 / bs)
        km_ref[j:j + 1, :] = jnp.where(lane1 < MOBA_DH, km, 0.0)
        km_ref[nb + j:nb + j + 1, :] = jnp.where(lane1 >= MOBA_DH, km, 0.0)
    km_hi, km_lo = _split2(km_ref[...])
    aux_row = lax.broadcasted_iota(jnp.int32, (n_aux, pw), 0)
    t_lo = (lax.broadcasted_iota(jnp.int32, (n_aux, pw), 1) % bs).astype(F32)
    for sub in range(2):
        slope = slopes_ref[2 * hp + sub]
        aux = jnp.where(aux_row == 0, -slope * t_lo, jnp.where(aux_row == 1, slope, 0.0)).astype(BF16)
        for u in range(n_pairs):
            qaug_ref[sub, u, LANES + nb:2 * LANES, :] = aux

    jrow = lax.broadcasted_iota(jnp.int32, (nb, pw), 0)
    later_block = lax.broadcasted_iota(jnp.int32, (nb, pw), 1) // bs
    drow = lax.broadcasted_iota(jnp.int32, (LANES, pw), 0)

    def prepare_queries(u):
        qt = qt_ref[u]
        qblk = 2 * u + later_block
        gate_both = _dot(km_hi, qt) + _dot(km_lo, qt)
        for sub in range(2):
            slope = slopes_ref[2 * hp + sub]
            gate = jnp.where(jrow < qblk, gate_both[sub * nb:(sub + 1) * nb], -jnp.inf)
            rank = jnp.zeros((nb, pw), jnp.int32)
            for jp in range(nb):
                g_jp = gate[jp:jp + 1, :]
                tie = jnp.where(g_jp == gate, jnp.where(jrow > jp, 1, 0), 0)
                rank = rank + jnp.where(g_jp > gate, 1, tie)
            rank = jnp.where(jrow < qblk, rank, jnp.where(jrow == qblk, 0, MOBA_TOPK))
            dist_blocks = (qblk - jrow).astype(F32) * float(bs)
            bias = jnp.where(rank < MOBA_TOPK, -slope * dist_blocks, NEG).astype(BF16)
            in_head = (drow < MOBA_DH) if sub == 0 else (drow >= MOBA_DH)
            qaug_ref[sub, u, 0:LANES, :] = jnp.where(in_head, qt, jnp.zeros_like(qt))
            qaug_ref[sub, u, LANES:LANES + nb, :] = bias

    prepare_queries(0)

    causal = (lax.broadcasted_iota(jnp.int32, (pw, pw), 0)
              <= lax.broadcasted_iota(jnp.int32, (pw, pw), 1))

    def attend(u, carry):
        prepare_queries(jnp.minimum(u + 1, n_pairs - 1))
        qa = [qaug_ref[sub, u] for sub in range(2)]

        def scores(jj):
            kj = kaug_ref[jj]
            return [_dot(kj, qa[sub]) for sub in range(2)]

        ss = [jnp.where(causal, s, NEG) for s in scores(u)]
        vj = vt_ref[u]
        ms = [jnp.max(s, axis=0, keepdims=True) for s in ss]
        ps = [jnp.exp(s - m) for s, m in zip(ss, ms)]
        state = []
        for sub in range(2):
            state += [ms[sub], jnp.sum(ps[sub], axis=0, keepdims=True), _dot(vj, ps[sub].astype(BF16))]

        def past_pair(jj, st):
            ss = scores(jj)
            vj = vt_ref[jj]
            ps, ms, ls, als = [], [], [], []
            for sub in range(2):
                m, l, _ = st[3 * sub:3 * sub + 3]
                s = ss[sub]
                m_new = jnp.maximum(m, jnp.max(s, axis=0, keepdims=True))
                alpha = jnp.exp(m - m_new)
                p = jnp.exp(s - m_new)
                ls.append(alpha * l + jnp.sum(p, axis=0, keepdims=True))
                ps.append(p.astype(BF16))
                ms.append(m_new)
                als.append(alpha)
            pvs = [_dot(vj, ps[sub]) for sub in range(2)]
            new = []
            for sub in range(2):
                new += [ms[sub], ls[sub], als[sub] * st[3 * sub + 2] + pvs[sub]]
            return tuple(new)

        st = lax.fori_loop(0, u, past_pair, tuple(state))
        ot_ref[u] = jnp.where(drow < MOBA_DH, st[2] / st[1], st[5] / st[4]).astype(BF16)
        return carry

    lax.fori_loop(0, n_pairs, attend, 0)
    for u in range(n_pairs):
        o_ref[u * pw:(u + 1) * pw, :] = jnp.transpose(ot_ref[u])


def _moba(slopes, q, k, v, kaux, batch, seq):
    nb = seq // MOBA_BLOCK
    n_pairs = MOBA_HEADS // 2
    blk = lambda b, h, s: (b, h)
    return pl.pallas_call(
        functools.partial(_moba_kernel, n_blocks=nb),
        grid_spec=pltpu.PrefetchScalarGridSpec(
            num_scalar_prefetch=1,
            grid=(batch, n_pairs),
            in_specs=[
                pl.BlockSpec((seq, LANES), blk),
                pl.BlockSpec((seq, LANES), blk),
                pl.BlockSpec((seq, LANES), blk),
                pl.BlockSpec((seq, LANES), lambda b, h, s: (0, 0)),
            ],
            out_specs=pl.BlockSpec((seq, LANES), blk),
            scratch_shapes=[
                pltpu.VMEM((nb // 2, 2 * MOBA_BLOCK, 2 * LANES), BF16),
                pltpu.VMEM((nb // 2, LANES, 2 * MOBA_BLOCK), BF16),
                pltpu.VMEM((nb // 2, LANES, 2 * MOBA_BLOCK), BF16),
                pltpu.VMEM((2, nb // 2, 2 * LANES, 2 * MOBA_BLOCK), BF16),
                pltpu.VMEM((2 * nb, LANES), F32),
                pltpu.VMEM((nb // 2, LANES, 2 * MOBA_BLOCK), BF16),
            ],
        ),
        out_shape=jax.ShapeDtypeStruct((batch * seq, MOBA_W), BF16),
        compiler_params=pltpu.CompilerParams(
            dimension_semantics=("parallel", "parallel"), vmem_limit_bytes=VMEM_LIMIT),
        name="moba",
    )(slopes, q, k, v, kaux)


def _post_kernel(*refs, n_parts, tf):
    x_ref = refs[0]
    part_refs = refs[1:1 + n_parts]
    wo_ref, g_ref, w1_ref, w2_ref, o_ref, a_ref = refs[1 + n_parts:]
    x = x_ref[...]
    row = 0
    for p_ref in part_refs:
        width = p_ref.shape[1]
        x = x + _dot(p_ref[...], wo_ref[row:row + width, :])
        row += width
    h = _rmsnorm_rows(x, g_ref[...]).astype(BF16)
    for c in range(D_FF // tf):
        cols = slice(c * tf, (c + 1) * tf)
        a = jnp.maximum(_dot(h, w1_ref[:, cols]), 0.0)
        a_ref[:, cols] = (a * a).astype(BF16)
    o_ref[...] = x + _dot(a_ref[...], w2_ref[...])


def _post(x2, parts, wo, g, w1, w2, tm, tf):
    m = x2.shape[0]
    const = lambda i: (0, 0)
    row = lambda i: (i, 0)
    resident = dict(pipeline_mode=pl.Buffered(1))
    return pl.pallas_call(
        functools.partial(_post_kernel, n_parts=len(parts), tf=tf),
        grid=(m // tm,),
        in_specs=(
            [pl.BlockSpec((tm, D_MODEL), row)]
            + [pl.BlockSpec((tm, p.shape[1]), row) for p in parts]
            + [pl.BlockSpec((D_MODEL, D_MODEL), const, **resident),
               pl.BlockSpec((1, D_MODEL), const),
               pl.BlockSpec((D_MODEL, D_FF), const, **resident),
               pl.BlockSpec((D_FF, D_MODEL), const, **resident)]),
        out_specs=pl.BlockSpec((tm, D_MODEL), row),
        out_shape=jax.ShapeDtypeStruct((m, D_MODEL), F32),
        scratch_shapes=[pltpu.VMEM((tm, D_FF), BF16)],
        compiler_params=pltpu.CompilerParams(
            dimension_semantics=("parallel",), vmem_limit_bytes=POST_VMEM_LIMIT),
        name="post",
    )(x2, *parts, wo, g, w1, w2)


def _odd_kernel(x_ref, g_ref, win_ref, cw_ref, pw_ref, ps_ref, mix_ref, zbuf_ref, ubuf_ref, *, tile):
    t_idx = pl.program_id(1)
    n_sub = 4
    sub_rows = tile // n_sub

    @pl.when(t_idx == 0)
    def _():
        zbuf_ref[0:HALO, :] = jnp.zeros((HALO, CONV_CH), F32)
        ubuf_ref[0:HALO, :] = jnp.zeros((HALO, POOL_CH), F32)

    proj = []
    for s in range(n_sub):
        rows = slice(s * sub_rows, (s + 1) * sub_rows)
        h = _rmsnorm_rows(x_ref[rows, :], g_ref[...]).astype(BF16)
        cg = _dot(h, win_ref[:, CONV_CH:2 * CONV_CH])
        xc = _dot(h, win_ref[:, 2 * CONV_CH:3 * CONV_CH])
        u = _dot(h, win_ref[:, 3 * CONV_CH:3 * CONV_CH + POOL_CH])
        bg = _dot(h, win_ref[:, 0:CONV_CH])
        proj.append((bg, cg, xc, u))

    for s in range(n_sub):
        bg, cg, xc, u = proj[s]
        rows = slice(s * sub_rows, (s + 1) * sub_rows)
        base = HALO + s * sub_rows
        z = cg * xc
        zbuf_ref[base:base + sub_rows, :] = z
        y = z * cw_ref[CONV_K - 1:CONV_K, :]
        for tap in range(CONV_K - 1):
            shift = CONV_K - 1 - tap
            y = y + zbuf_ref[base - shift:base - shift + sub_rows, :] * cw_ref[tap:tap + 1, :]
        mix_ref[rows, 0:CONV_CH] = (bg * y).astype(BF16)

        ubuf_ref[base:base + sub_rows, :] = u
        pos = (lax.broadcasted_iota(jnp.int32, (sub_rows, POOL_GROUP), 0)
               + (t_idx * tile + s * sub_rows))
        for grp, win in enumerate(POOL_WINDOWS):
            cs = slice(grp * POOL_GROUP, (grp + 1) * POOL_GROUP)
            u_g = u[:, cs]
            tot = u_g
            for shift in range(1, win):
                tot = tot + ubuf_ref[base - shift:base - shift + sub_rows, cs]
            cnt = jnp.minimum(pos + 1, win).astype(F32)
            pooled = (tot / cnt - u_g).astype(BF16)
            y_g = _dot(pooled, pw_ref[grp]) * ps_ref[:, cs]
            mix_ref[rows, CONV_CH + grp * POOL_GROUP:CONV_CH + (grp + 1) * POOL_GROUP] = y_g.astype(BF16)

    zbuf_ref[0:HALO, :] = zbuf_ref[tile:tile + HALO, :]
    ubuf_ref[0:HALO, :] = ubuf_ref[tile:tile + HALO, :]


def _odd_mix(x2, g, win, cw, pw, ps, batch, seq, tile):
    nt = seq // tile
    const = lambda b, t: (0, 0)
    row = lambda b, t: (b * nt + t, 0)
    n_in = 3 * CONV_CH + POOL_CH
    return pl.pallas_call(
        functools.partial(_odd_kernel, tile=tile),
        grid=(batch, nt),
        in_specs=[
            pl.BlockSpec((tile, D_MODEL), row),
            pl.BlockSpec((1, D_MODEL), const),
            pl.BlockSpec((D_MODEL, n_in), const),
            pl.BlockSpec((CONV_K, CONV_CH), const),
            pl.BlockSpec((len(POOL_WINDOWS), POOL_GROUP, POOL_GROUP), lambda b, t: (0, 0, 0)),
            pl.BlockSpec((1, POOL_CH), const),
        ],
        out_specs=pl.BlockSpec((tile, CONV_CH + POOL_CH), row),
        out_shape=jax.ShapeDtypeStruct((batch * seq, CONV_CH + POOL_CH), BF16),
        scratch_shapes=[
            pltpu.VMEM((tile + HALO, CONV_CH), F32),
            pltpu.VMEM((tile + HALO, POOL_CH), F32),
        ],
        compiler_params=pltpu.CompilerParams(
            dimension_semantics=("parallel", "arbitrary"), vmem_limit_bytes=VMEM_LIMIT),
        name="odd_mix",
    )(x2, g, win, cw, pw, ps)


def _gla_tri(tile):
    r = np.arange(tile)
    same = (r[:, None] // GLA_CHUNK) == (r[None, :] // GLA_CHUNK)
    lower = same & (r[None, :] <= r[:, None])
    return jnp.asarray(lower.astype(np.float32), dtype=BF16)


def _moba_kaux(seq):
    pos = np.arange(seq)
    nb = seq // MOBA_BLOCK
    aux = np.zeros((seq, LANES), np.float32)
    aux[pos, pos // MOBA_BLOCK] = 1.0
    aux[:, nb] = 1.0
    aux[:, nb + 1] = pos % MOBA_BLOCK
    return jnp.asarray(aux, dtype=BF16)


def _head_group_matrix():
    r = np.arange(MOBA_W)
    same = (r[:, None] // MOBA_DH) == (r[None, :] // MOBA_DH)
    return jnp.asarray(same.astype(np.float32) / MOBA_DH, dtype=BF16)


def _even_weight(w_in):
    qa, ka, va, ra, glr, qb, kb, vb = jnp.split(
        w_in, np.cumsum([GLA_QK, GLA_QK, GLA_V, GLA_V, GLA_GATE_RANK, MOBA_W, MOBA_W])[:].tolist(), axis=-1)
    glr_pad = jnp.pad(glr, ((0, 0), (0, LANES - GLA_GATE_RANK)))
    return jnp.concatenate([qa, ka, ra, glr_pad, va, qb, kb, vb], axis=-1).astype(BF16)


def kernel(x, norm_mix, norm_mlp, w_o, w1, w2, even_w_in, gla_w_gate2, gla_b_gate, gla_out_norm,
           moba_q_norm, moba_k_norm, odd_w_in, conv_w, pool_w, pool_scale):
    batch, seq, d = x.shape
    assert d == D_MODEL and seq % (4 * MOBA_BLOCK) == 0
    assert seq // MOBA_BLOCK + 2 <= LANES and seq // MOBA_BLOCK % 16 == 0
    depth = norm_mix.shape[0]
    m = batch * seq
    tm = 1024
    gla_tile = 512
    gla_sub = 256
    odd_tile = 1024
    post_tm = 1024
    post_tf = 512

    tri = _gla_tri(gla_sub)
    kaux = _moba_kaux(seq)
    grp = _head_group_matrix()
    slopes = jnp.exp2(-8.0 * jnp.arange(1, MOBA_HEADS + 1, dtype=F32) / MOBA_HEADS)

    x2 = x.reshape(m, d)
    for i in range(depth):
        j = i // 2
        g_mix = norm_mix[i].reshape(1, d)
        wo = w_o[i].astype(BF16)
        if i % 2 == 0:
            w_in = _even_weight(even_w_in[j])
            gq = (jnp.tile(moba_q_norm[j], MOBA_HEADS) * (MOBA_DH ** -0.5)).reshape(1, MOBA_W)
            gk = jnp.tile(moba_k_norm[j], MOBA_HEADS).reshape(1, MOBA_W)
            gla_in, va, qb, kb, vb = _even_inproj(x2, g_mix, w_in, gq, gk, grp, tm)
            wg = jnp.pad(gla_w_gate2[j], ((0, LANES - GLA_GATE_RANK), (0, 0))).astype(BF16)
            oa = _gla(gla_in, va, wg, gla_b_gate[j].reshape(1, GLA_QK),
                      gla_out_norm[j].reshape(1, GLA_DV), tri, batch, seq, gla_tile, gla_sub)
            ob = _moba(slopes, qb, kb, vb, kaux, batch, seq)
            parts = [oa, ob]
        else:
            parts = [_odd_mix(x2, g_mix, odd_w_in[j].astype(BF16), conv_w[j], pool_w[j].astype(BF16),
                              pool_scale[j].reshape(1, POOL_CH), batch, seq, odd_tile)]
        x2 = _post(x2, parts, wo, norm_mlp[i].reshape(1, d), w1[i].astype(BF16), w2[i].astype(BF16),
                   post_tm, post_tf)
    return x2.reshape(batch, seq, d)
```

```python
import functools

import numpy as np
import jax
import jax.numpy as jnp
from jax import lax
from jax.experimental import pallas as pl
from jax.experimental.pallas import tpu as pltpu

F32 = jnp.float32
BF16 = jnp.bfloat16

D_MODEL = 1024
GLA_HEADS = 4
GLA_DK = 64
GLA_DV = 128
GLA_GATE_RANK = 16
GLA_GATE_TEMP = 16.0
GLA_CHUNK = 64
MOBA_HEADS = 8
MOBA_DH = 64
MOBA_BLOCK = 256
MOBA_TOPK = 3
CONV_CH = 512
CONV_K = 3
POOL_CH = 512
POOL_WINDOWS = (2, 4, 8, 16)
POOL_GROUP = 128
D_FF = 4096
EPS = 1e-6

GLA_QK = GLA_HEADS * GLA_DK
GLA_V = GLA_HEADS * GLA_DV
MOBA_W = MOBA_HEADS * MOBA_DH
LANES = 128
HALO = 16
NEG = -1e30
LOG2E = 1.4426950408889634
N_TERMS = 3

GLA_F32_W = GLA_QK + GLA_QK + GLA_V + LANES
EVEN_N = GLA_F32_W + GLA_V + 3 * MOBA_W

VMEM_LIMIT = 48 * 1024 * 1024
POST_VMEM_LIMIT = 56 * 1024 * 1024

NT_DIMS = (((1,), (1,)), ((), ()))
TN_DIMS = (((0,), (0,)), ((), ()))


def _dot(a, b):
    return jnp.dot(a, b, preferred_element_type=F32)


def _dot_nt(a, b):
    return lax.dot_general(a, b, NT_DIMS, preferred_element_type=F32)


def _dot_tn(a, b):
    return lax.dot_general(a, b, TN_DIMS, preferred_element_type=F32)


def _rmsnorm_rows(x, g):
    ms = jnp.mean(x * x, axis=-1, keepdims=True)
    return x * lax.rsqrt(ms + EPS) * g


def _split2(y):
    hi = y.astype(BF16)
    lo = (y - hi.astype(F32)).astype(BF16)
    return hi, lo


def _split_terms(y, n):
    terms = []
    for _ in range(n):
        t = y.astype(BF16)
        terms.append(t)
        y = y - t.astype(F32)
    return terms


def _even_inproj_kernel(x_ref, g_ref, w_ref, gq_ref, gk_ref, grp_ref,
                        gla_ref, va_ref, qb_ref, kb_ref, vb_ref, *, n_sub):
    c0 = GLA_F32_W
    c1 = c0 + GLA_V
    c2 = c1 + MOBA_W
    c3 = c2 + MOBA_W
    sub_rows = x_ref.shape[0] // n_sub

    def headnorm(y, gain):
        ms = _dot((y * y).astype(BF16), grp_ref[...])
        return (y * lax.rsqrt(ms + EPS) * gain).astype(BF16)

    for s in range(n_sub):
        rows = slice(s * sub_rows, (s + 1) * sub_rows)
        h = _rmsnorm_rows(x_ref[rows, :], g_ref[...]).astype(BF16)
        qb = _dot(h, w_ref[:, c1:c2])
        kb = _dot(h, w_ref[:, c2:c3])
        gla_ref[rows, :] = _dot(h, w_ref[:, 0:c0])
        va_ref[rows, :] = _dot(h, w_ref[:, c0:c1]).astype(BF16)
        vb_ref[rows, :] = _dot(h, w_ref[:, c3:c3 + MOBA_W]).astype(BF16)
        qb_ref[rows, :] = headnorm(qb, gq_ref[...])
        kb_ref[rows, :] = headnorm(kb, gk_ref[...])


def _even_inproj(x2, g, w, gq, gk, grp, tm):
    m = x2.shape[0]
    const = lambda i: (0, 0)
    row = lambda i: (i, 0)
    return pl.pallas_call(
        functools.partial(_even_inproj_kernel, n_sub=2),
        grid=(m // tm,),
        in_specs=[
            pl.BlockSpec((tm, D_MODEL), row),
            pl.BlockSpec((1, D_MODEL), const),
            pl.BlockSpec((D_MODEL, EVEN_N), const, pipeline_mode=pl.Buffered(1)),
            pl.BlockSpec((1, MOBA_W), const),
            pl.BlockSpec((1, MOBA_W), const),
            pl.BlockSpec((MOBA_W, MOBA_W), const),
        ],
        out_specs=[
            pl.BlockSpec((tm, GLA_F32_W), row),
            pl.BlockSpec((tm, GLA_V), row),
            pl.BlockSpec((tm, MOBA_W), row),
            pl.BlockSpec((tm, MOBA_W), row),
            pl.BlockSpec((tm, MOBA_W), row),
        ],
        out_shape=[
            jax.ShapeDtypeStruct((m, GLA_F32_W), F32),
            jax.ShapeDtypeStruct((m, GLA_V), BF16),
            jax.ShapeDtypeStruct((m, MOBA_W), BF16),
            jax.ShapeDtypeStruct((m, MOBA_W), BF16),
            jax.ShapeDtypeStruct((m, MOBA_W), BF16),
        ],
        compiler_params=pltpu.CompilerParams(
            dimension_semantics=("parallel",), vmem_limit_bytes=VMEM_LIMIT),
        name="even_inproj",
    )(x2, g, w, gq, gk, grp)


def _gla_kernel(gla_ref, va_ref, wg_ref, bg_ref, gn_ref, tri_ref, o_ref, state_ref, *, sub_tile):
    n_sub = gla_ref.shape[0] // sub_tile
    n_chunks = sub_tile // GLA_CHUNK
    heads = range(GLA_HEADS)
    chunks = [slice(c * GLA_CHUNK, (c + 1) * GLA_CHUNK) for c in range(n_chunks)]
    r0 = 2 * GLA_QK
    tri = tri_ref[...]
    causal = tri > 0
    lane = lax.broadcasted_iota(jnp.int32, (sub_tile, LANES), 1)

    @pl.when(pl.program_id(1) == 0)
    def _():
        state_ref[...] = jnp.zeros_like(state_ref)

    def pair_slice(head):
        return slice((head // 2) * LANES, (head // 2 + 1) * LANES)

    def in_head(head):
        return (lane < GLA_DK) if head % 2 == 0 else (lane >= GLA_DK)

    pre = []
    for s in range(n_sub):
        rows = slice(s * sub_tile, (s + 1) * sub_tile)
        glr = gla_ref[rows, r0 + GLA_V:r0 + GLA_V + LANES].astype(BF16)
        z = _dot(glr, wg_ref[...]) + bg_ref[...]
        log_a = (jnp.minimum(z, 0.0) - jnp.log1p(jnp.exp(-jnp.abs(z)))) * (1.0 / GLA_GATE_TEMP)
        g_hi, g_lo = _split2(log_a)
        bb = _dot(tri, jnp.concatenate([g_hi, g_lo], axis=1))
        b = bb[:, 0:GLA_QK] + bb[:, GLA_QK:2 * GLA_QK]
        b_last = [b[(c + 1) * GLA_CHUNK - 1:(c + 1) * GLA_CHUNK, :] for c in range(n_chunks)]
        btot = jnp.concatenate([jnp.broadcast_to(bl, (GLA_CHUNK, GLA_QK)) for bl in b_last], axis=0)
        q_ = gla_ref[rows, 0:GLA_QK] * (GLA_DK ** -0.5) * jnp.exp(b)
        k = gla_ref[rows, GLA_QK:2 * GLA_QK]
        k_ = k * jnp.exp(-b)
        k_end = k * jnp.exp(btot - b)
        qm = [jnp.where(in_head(h), q_[:, pair_slice(h)], 0.0).astype(BF16) for h in heads]
        kem = [jnp.where(in_head(h), k_end[:, pair_slice(h)], 0.0).astype(BF16) for h in heads]
        v = [va_ref[rows, h * GLA_DV:(h + 1) * GLA_DV] for h in heads]
        d_state = [[_dot_tn(v[h][cr], kem[h][cr]) for cr in chunks] for h in heads]
        att = [jnp.where(causal, _dot_nt(qm[h], k_[:, pair_slice(h)].astype(BF16)), 0.0).astype(BF16)
               for h in heads]
        o_intra = [_dot(att[h], v[h]) for h in heads]
        decay = [[jnp.exp(bl[:, pair_slice(h)]) for bl in b_last] for h in heads]
        pre.append((qm, d_state, o_intra, decay))

    states = [[] for _ in range(n_sub)]
    for h in heads:
        st = state_ref[h]
        for s in range(n_sub):
            _, d_state, _, decay = pre[s]
            per_chunk = []
            for c in range(n_chunks):
                per_chunk.append(st.astype(BF16))
                st = st * decay[h][c] + d_state[h][c]
            states[s].append(per_chunk)
        state_ref[h] = st

    for s in range(n_sub):
        rows = slice(s * sub_tile, (s + 1) * sub_tile)
        qm, _, o_intra, _ = pre[s]
        for h in heads:
            o_inter = jnp.concatenate(
                [_dot_nt(qm[h][cr], states[s][h][c]) for c, cr in enumerate(chunks)], axis=0)
            on = _rmsnorm_rows(o_intra[h] + o_inter, gn_ref[...])
            r = gla_ref[rows, r0 + h * GLA_DV:r0 + (h + 1) * GLA_DV]
            o_ref[rows, h * GLA_DV:(h + 1) * GLA_DV] = (on * (r * jax.nn.sigmoid(r))).astype(BF16)


def _gla(gla_in, va, wg, bg, gn, tri, batch, seq, tile, sub_tile):
    nt = seq // tile
    const = lambda b, t: (0, 0)
    row = lambda b, t: (b * nt + t, 0)
    return pl.pallas_call(
        functools.partial(_gla_kernel, sub_tile=sub_tile),
        grid=(batch, nt),
        in_specs=[
            pl.BlockSpec((tile, GLA_F32_W), row),
            pl.BlockSpec((tile, GLA_V), row),
            pl.BlockSpec((LANES, GLA_QK), const),
            pl.BlockSpec((1, GLA_QK), const),
            pl.BlockSpec((1, GLA_DV), const),
            pl.BlockSpec((sub_tile, sub_tile), const),
        ],
        out_specs=pl.BlockSpec((tile, GLA_V), row),
        out_shape=jax.ShapeDtypeStruct((batch * seq, GLA_V), BF16),
        scratch_shapes=[pltpu.VMEM((GLA_HEADS, GLA_DV, LANES), F32)],
        compiler_params=pltpu.CompilerParams(
            dimension_semantics=("parallel", "arbitrary"), vmem_limit_bytes=VMEM_LIMIT),
        name="gla",
    )(gla_in, va, wg, bg, gn, tri)


def _moba_kernel(slopes_ref, q_ref, k_ref, v_ref, kaux_ref, o_ref,
                 kaug_ref, vt_ref, qt_ref, qaug_ref, km_ref, *, n_blocks):
    bs = MOBA_BLOCK
    nb = n_blocks
    n_pairs = nb // 2
    pw = 2 * bs
    bias_rows = N_TERMS * nb
    n_aux = LANES - bias_rows
    hp = pl.program_id(1)
    lane1 = lax.broadcasted_iota(jnp.int32, (1, LANES), 1)

    for jj in range(n_pairs):
        rows = slice(jj * pw, (jj + 1) * pw)
        kaug_ref[jj, :, 0:LANES] = k_ref[rows, :]
        kaug_ref[jj, :, LANES:2 * LANES] = kaux_ref[rows, :]
        vt = jnp.transpose(v_ref[rows, :])
        vrow = lax.broadcasted_iota(jnp.int32, (LANES, pw), 0)
        vt_ref[0, jj] = jnp.where(vrow < MOBA_DH, vt, jnp.ones_like(vt))
        vt_ref[1, jj] = jnp.where(vrow >= MOBA_DH, vt, jnp.ones_like(vt))
        qt_ref[jj] = jnp.transpose(q_ref[rows, :])
    for j in range(nb):
        km = jnp.sum(k_ref[j * bs:(j + 1) * bs, :].astype(F32), axis=0, keepdims=True) * (1.0 / bs)
        km_ref[j:j + 1, :] = jnp.where(lane1 < MOBA_DH, km, 0.0)
        km_ref[nb + j:nb + j + 1, :] = jnp.where(lane1 >= MOBA_DH, km, 0.0)
    km_hi, km_lo = _split2(km_ref[...])
    aux_row = lax.broadcasted_iota(jnp.int32, (n_aux, pw), 0)
    t_lo = (lax.broadcasted_iota(jnp.int32, (n_aux, pw), 1) % bs).astype(F32)
    for sub in range(2):
        slope2 = slopes_ref[2 * hp + sub] * LOG2E
        rows = _split_terms(-slope2 * t_lo, N_TERMS) + _split_terms(jnp.full((n_aux, pw), slope2, F32), N_TERMS)
        aux = jnp.zeros((n_aux, pw), F32)
        for r, term in enumerate(rows):
            aux = jnp.where(aux_row == r, term.astype(F32), aux)
        aux = aux.astype(BF16)
        for u in range(n_pairs):
            qaug_ref[sub, u, LANES + bias_rows:2 * LANES, :] = aux

    jrow = lax.broadcasted_iota(jnp.int32, (nb, pw), 0)
    later_block = lax.broadcasted_iota(jnp.int32, (nb, pw), 1) // bs
    drow = lax.broadcasted_iota(jnp.int32, (LANES, pw), 0)

    for u in range(n_pairs):
        qt = qt_ref[u]
        qblk = 2 * u + later_block
        gate_both = _dot(km_hi, qt) + _dot(km_lo, qt)
        for sub in range(2):
            slope2 = slopes_ref[2 * hp + sub] * LOG2E
            gate = jnp.where(jrow < qblk, gate_both[sub * nb:(sub + 1) * nb], -jnp.inf)
            rank = jnp.zeros((nb, pw), jnp.int32)
            for jp in range(2 * u + 1):
                g_jp = gate[jp:jp + 1, :]
                tie = jnp.where(g_jp == gate, jnp.where(jrow > jp, 1, 0), 0)
                rank = rank + jnp.where(g_jp > gate, 1, tie)
            rank = jnp.where(jrow < qblk, rank, jnp.where(jrow == qblk, 0, MOBA_TOPK))
            dist_blocks = (qblk - jrow).astype(F32) * float(bs)
            bias = jnp.where(rank < MOBA_TOPK, -slope2 * dist_blocks, NEG)
            in_head = (drow < MOBA_DH) if sub == 0 else (drow >= MOBA_DH)
            qaug_ref[sub, u, 0:LANES, :] = jnp.where(in_head, qt, jnp.zeros_like(qt))
            for r, term in enumerate(_split_terms(bias, N_TERMS)):
                qaug_ref[sub, u, LANES + r * nb:LANES + (r + 1) * nb, :] = term

    causal = (lax.broadcasted_iota(jnp.int32, (pw, pw), 0)
              <= lax.broadcasted_iota(jnp.int32, (pw, pw), 1))

    def score(u, jj, sub):
        return _dot(kaug_ref[jj], qaug_ref[sub, u])

    def update(m, acc, s, jj, sub):
        m_new = jnp.maximum(m, jnp.max(s, axis=0, keepdims=True))
        p = jnp.exp2(s - m_new).astype(BF16)
        return m_new, jnp.exp2(m - m_new) * acc + _dot(vt_ref[sub, jj], p)

    for u in range(n_pairs):
        steps = [(jj, sub) for jj in [u] + list(range(u)) for sub in range(2)]
        st = [None, None]
        s_cur = score(u, *steps[0])
        for n, (jj, sub) in enumerate(steps):
            s_next = score(u, *steps[n + 1]) if n + 1 < len(steps) else None
            if jj == u:
                s_cur = jnp.where(causal, s_cur, NEG)
                m = jnp.max(s_cur, axis=0, keepdims=True)
                st[sub] = (m, _dot(vt_ref[sub, jj], jnp.exp2(s_cur - m).astype(BF16)))
            else:
                st[sub] = update(*st[sub], s_cur, jj, sub)
            s_cur = s_next
        acc0, acc1 = st[0][1], st[1][1]
        out_t = jnp.where(drow < MOBA_DH, acc0 / acc0[MOBA_DH:MOBA_DH + 1], acc1 / acc1[0:1]).astype(BF16)
        o_ref[u * pw:(u + 1) * pw, :] = jnp.transpose(out_t)


def _moba(slopes, q, k, v, kaux, batch, seq):
    nb = seq // MOBA_BLOCK
    n_pairs = MOBA_HEADS // 2
    blk = lambda b, h, s: (b, h)
    return pl.pallas_call(
        functools.partial(_moba_kernel, n_blocks=nb),
        grid_spec=pltpu.PrefetchScalarGridSpec(
            num_scalar_prefetch=1,
            grid=(batch, n_pairs),
            in_specs=[
                pl.BlockSpec((seq, LANES), blk),
                pl.BlockSpec((seq, LANES), blk),
                pl.BlockSpec((seq, LANES), blk),
                pl.BlockSpec((seq, LANES), lambda b, h, s: (0, 0)),
            ],
            out_specs=pl.BlockSpec((seq, LANES), blk),
            scratch_shapes=[
                pltpu.VMEM((nb // 2, 2 * MOBA_BLOCK, 2 * LANES), BF16),
                pltpu.VMEM((2, nb // 2, LANES, 2 * MOBA_BLOCK), BF16),
                pltpu.VMEM((nb // 2, LANES, 2 * MOBA_BLOCK), BF16),
                pltpu.VMEM((2, nb // 2, 2 * LANES, 2 * MOBA_BLOCK), BF16),
                pltpu.VMEM((2 * nb, LANES), F32),
            ],
        ),
        out_shape=jax.ShapeDtypeStruct((batch * seq, MOBA_W), BF16),
        compiler_params=pltpu.CompilerParams(
            dimension_semantics=("parallel", "parallel"), vmem_limit_bytes=VMEM_LIMIT),
        name="moba",
    )(slopes, q, k, v, kaux)


def _post_kernel(*refs, n_parts, tf):
    x_ref = refs[0]
    part_refs = refs[1:1 + n_parts]
    wo_ref, g_ref, w1_ref, w2_ref, o_ref, a_ref = refs[1 + n_parts:]
    x = x_ref[...]
    row = 0
    for p_ref in part_refs:
        width = p_ref.shape[1]
        x = x + _dot(p_ref[...], wo_ref[row:row + width, :])
        row += width
    h = _rmsnorm_rows(x, g_ref[...]).astype(BF16)
    for c in range(D_FF // tf):
        cols = slice(c * tf, (c + 1) * tf)
        a = jnp.maximum(_dot(h, w1_ref[:, cols]), 0.0)
        a_ref[:, cols] = (a * a).astype(BF16)
    o_ref[...] = x + _dot(a_ref[...], w2_ref[...])


def _post(x2, parts, wo, g, w1, w2, tm, tf):
    m = x2.shape[0]
    const = lambda i: (0, 0)
    row = lambda i: (i, 0)
    resident = dict(pipeline_mode=pl.Buffered(1))
    return pl.pallas_call(
        functools.partial(_post_kernel, n_parts=len(parts), tf=tf),
        grid=(m // tm,),
        in_specs=(
            [pl.BlockSpec((tm, D_MODEL), row)]
            + [pl.BlockSpec((tm, p.shape[1]), row) for p in parts]
            + [pl.BlockSpec((D_MODEL, D_MODEL), const, **resident),
               pl.BlockSpec((1, D_MODEL), const),
               pl.BlockSpec((D_MODEL, D_FF), const, **resident),
               pl.BlockSpec((D_FF, D_MODEL), const, **resident)]),
        out_specs=pl.BlockSpec((tm, D_MODEL), row),
        out_shape=jax.ShapeDtypeStruct((m, D_MODEL), F32),
        scratch_shapes=[pltpu.VMEM((tm, D_FF), BF16)],
        compiler_params=pltpu.CompilerParams(
            dimension_semantics=("parallel",), vmem_limit_bytes=POST_VMEM_LIMIT),
        name="post",
    )(x2, *parts, wo, g, w1, w2)


def _odd_kernel(x_ref, g_ref, win_ref, cw_ref, pw_ref, ps_ref, mix_ref, zbuf_ref, ubuf_ref, *, tile):
    t_idx = pl.program_id(1)
    n_sub = 4
    sub_rows = tile // n_sub

    @pl.when(t_idx == 0)
    def _():
        zbuf_ref[0:HALO, :] = jnp.zeros((HALO, CONV_CH), F32)
        ubuf_ref[0:HALO, :] = jnp.zeros((HALO, POOL_CH), F32)

    proj = []
    for s in range(n_sub):
        rows = slice(s * sub_rows, (s + 1) * sub_rows)
        h = _rmsnorm_rows(x_ref[rows, :], g_ref[...]).astype(BF16)
        cg = _dot(h, win_ref[:, CONV_CH:2 * CONV_CH])
        xc = _dot(h, win_ref[:, 2 * CONV_CH:3 * CONV_CH])
        u = _dot(h, win_ref[:, 3 * CONV_CH:3 * CONV_CH + POOL_CH])
        bg = _dot(h, win_ref[:, 0:CONV_CH])
        proj.append((bg, cg, xc, u))

    for s in range(n_sub):
        bg, cg, xc, u = proj[s]
        rows = slice(s * sub_rows, (s + 1) * sub_rows)
        base = HALO + s * sub_rows
        z = cg * xc
        zbuf_ref[base:base + sub_rows, :] = z
        y = z * cw_ref[CONV_K - 1:CONV_K, :]
        for tap in range(CONV_K - 1):
            shift = CONV_K - 1 - tap
            y = y + zbuf_ref[base - shift:base - shift + sub_rows, :] * cw_ref[tap:tap + 1, :]
        mix_ref[rows, 0:CONV_CH] = (bg * y).astype(BF16)

        ubuf_ref[base:base + sub_rows, :] = u
        pos = (lax.broadcasted_iota(jnp.int32, (sub_rows, POOL_GROUP), 0)
               + (t_idx * tile + s * sub_rows))
        for grp, win in enumerate(POOL_WINDOWS):
            cs = slice(grp * POOL_GROUP, (grp + 1) * POOL_GROUP)
            u_g = u[:, cs]
            tot = u_g
            for shift in range(1, win):
                tot = tot + ubuf_ref[base - shift:base - shift + sub_rows, cs]
            cnt = jnp.minimum(pos + 1, win).astype(F32)
            pooled = (tot / cnt - u_g).astype(BF16)
            y_g = _dot(pooled, pw_ref[grp]) * ps_ref[:, cs]
            mix_ref[rows, CONV_CH + grp * POOL_GROUP:CONV_CH + (grp + 1) * POOL_GROUP] = y_g.astype(BF16)

    zbuf_ref[0:HALO, :] = zbuf_ref[tile:tile + HALO, :]
    ubuf_ref[0:HALO, :] = ubuf_ref[tile:tile + HALO, :]


def _odd_mix(x2, g, win, cw, pw, ps, batch, seq, tile):
    nt = seq // tile
    const = lambda b, t: (0, 0)
    row = lambda b, t: (b * nt + t, 0)
    n_in = 3 * CONV_CH + POOL_CH
    return pl.pallas_call(
        functools.partial(_odd_kernel, tile=tile),
        grid=(batch, nt),
        in_specs=[
            pl.BlockSpec((tile, D_MODEL), row),
            pl.BlockSpec((1, D_MODEL), const),
            pl.BlockSpec((D_MODEL, n_in), const),
            pl.BlockSpec((CONV_K, CONV_CH), const),
            pl.BlockSpec((len(POOL_WINDOWS), POOL_GROUP, POOL_GROUP), lambda b, t: (0, 0, 0)),
            pl.BlockSpec((1, POOL_CH), const),
        ],
        out_specs=pl.BlockSpec((tile, CONV_CH + POOL_CH), row),
        out_shape=jax.ShapeDtypeStruct((batch * seq, CONV_CH + POOL_CH), BF16),
        scratch_shapes=[
            pltpu.VMEM((tile + HALO, CONV_CH), F32),
            pltpu.VMEM((tile + HALO, POOL_CH), F32),
        ],
        compiler_params=pltpu.CompilerParams(
            dimension_semantics=("parallel", "arbitrary"), vmem_limit_bytes=VMEM_LIMIT),
        name="odd_mix",
    )(x2, g, win, cw, pw, ps)


def _gla_tri(tile):
    r = np.arange(tile)
    same = (r[:, None] // GLA_CHUNK) == (r[None, :] // GLA_CHUNK)
    lower = same & (r[None, :] <= r[:, None])
    return jnp.asarray(lower.astype(np.float32), dtype=BF16)


def _moba_kaux(seq):
    pos = np.arange(seq)
    nb = seq // MOBA_BLOCK
    aux = np.zeros((seq, LANES), np.float32)
    for r in range(N_TERMS):
        aux[pos, r * nb + pos // MOBA_BLOCK] = 1.0
        aux[:, N_TERMS * nb + r] = 1.0
        aux[:, N_TERMS * (nb + 1) + r] = pos % MOBA_BLOCK
    return jnp.asarray(aux, dtype=BF16)


def _head_group_matrix():
    r = np.arange(MOBA_W)
    same = (r[:, None] // MOBA_DH) == (r[None, :] // MOBA_DH)
    return jnp.asarray(same.astype(np.float32) / MOBA_DH, dtype=BF16)


def _even_weight(w_in):
    qa, ka, va, ra, glr, qb, kb, vb = jnp.split(
        w_in, np.cumsum([GLA_QK, GLA_QK, GLA_V, GLA_V, GLA_GATE_RANK, MOBA_W, MOBA_W])[:].tolist(), axis=-1)
    glr_pad = jnp.pad(glr, ((0, 0), (0, LANES - GLA_GATE_RANK)))
    return jnp.concatenate([qa, ka, ra, glr_pad, va, qb, kb, vb], axis=-1).astype(BF16)


def kernel(x, norm_mix, norm_mlp, w_o, w1, w2, even_w_in, gla_w_gate2, gla_b_gate, gla_out_norm,
           moba_q_norm, moba_k_norm, odd_w_in, conv_w, pool_w, pool_scale):
    batch, seq, d = x.shape
    assert d == D_MODEL and seq % (4 * MOBA_BLOCK) == 0
    assert N_TERMS * (seq // MOBA_BLOCK + 2) <= LANES and seq // MOBA_BLOCK % 16 == 0
    depth = norm_mix.shape[0]
    m = batch * seq
    tm = 1024
    gla_tile = 1024
    gla_sub = 256
    odd_tile = 1024
    post_tm = 1024
    post_tf = 512

    tri = _gla_tri(gla_sub)
    kaux = _moba_kaux(seq)
    grp = _head_group_matrix()
    slopes = jnp.exp2(-8.0 * jnp.arange(1, MOBA_HEADS + 1, dtype=F32) / MOBA_HEADS)

    x2 = x.reshape(m, d)
    for i in range(depth):
        j = i // 2
        g_mix = norm_mix[i].reshape(1, d)
        wo = w_o[i].astype(BF16)
        if i % 2 == 0:
            w_in = _even_weight(even_w_in[j])
            gq = (jnp.tile(moba_q_norm[j], MOBA_HEADS) * (MOBA_DH ** -0.5 * LOG2E)).reshape(1, MOBA_W)
            gk = jnp.tile(moba_k_norm[j], MOBA_HEADS).reshape(1, MOBA_W)
            gla_in, va, qb, kb, vb = _even_inproj(x2, g_mix, w_in, gq, gk, grp, tm)
            wg = jnp.pad(gla_w_gate2[j], ((0, LANES - GLA_GATE_RANK), (0, 0))).astype(BF16)
            oa = _gla(gla_in, va, wg, gla_b_gate[j].reshape(1, GLA_QK),
                      gla_out_norm[j].reshape(1, GLA_DV), tri, batch, seq, gla_tile, gla_sub)
            ob = _moba(slopes, qb, kb, vb, kaux, batch, seq)
            parts = [oa, ob]
        else:
            parts = [_odd_mix(x2, g_mix, odd_w_in[j].astype(BF16), conv_w[j], pool_w[j].astype(BF16),
                              pool_scale[j].reshape(1, POOL_CH), batch, seq, odd_tile)]
        x2 = _post(x2, parts, wo, norm_mlp[i].reshape(1, d), w1[i].astype(BF16), w2[i].astype(BF16),
                   post_tm, post_tf)
    return x2.reshape(batch, seq, d)
```

```python
import functools

import numpy as np
import jax
import jax.numpy as jnp
from jax import lax
from jax.experimental import pallas as pl
from jax.experimental.pallas import tpu as pltpu

F32 = jnp.float32
BF16 = jnp.bfloat16

D_MODEL = 1024
GLA_HEADS = 4
GLA_DK = 64
GLA_DV = 128
GLA_GATE_RANK = 16
GLA_GATE_TEMP = 16.0
GLA_CHUNK = 64
MOBA_HEADS = 8
MOBA_DH = 64
MOBA_BLOCK = 256
MOBA_TOPK = 3
CONV_CH = 512
CONV_K = 3
POOL_CH = 512
POOL_WINDOWS = (2, 4, 8, 16)
POOL_GROUP = 128
D_FF = 4096
EPS = 1e-6

GLA_QK = GLA_HEADS * GLA_DK
GLA_V = GLA_HEADS * GLA_DV
MOBA_W = MOBA_HEADS * MOBA_DH
LANES = 128
HALO = 16
NEG = -1e30
LOG2E = 1.4426950408889634
N_TERMS = 3

GLA_F32_W = GLA_QK + GLA_QK + GLA_V + LANES
EVEN_N = GLA_F32_W + GLA_V + 3 * MOBA_W

VMEM_LIMIT = 48 * 1024 * 1024
POST_VMEM_LIMIT = 56 * 1024 * 1024

NT_DIMS = (((1,), (1,)), ((), ()))
TN_DIMS = (((0,), (0,)), ((), ()))


def _dot(a, b):
    return jnp.dot(a, b, preferred_element_type=F32)


def _dot_nt(a, b):
    return lax.dot_general(a, b, NT_DIMS, preferred_element_type=F32)


def _dot_tn(a, b):
    return lax.dot_general(a, b, TN_DIMS, preferred_element_type=F32)


def _rmsnorm_rows(x, g):
    ms = jnp.mean(x * x, axis=-1, keepdims=True)
    return x * lax.rsqrt(ms + EPS) * g


def _split2(y):
    hi = y.astype(BF16)
    lo = (y - hi.astype(F32)).astype(BF16)
    return hi, lo


def _split_terms(y, n):
    terms = []
    for _ in range(n):
        t = y.astype(BF16)
        terms.append(t)
        y = y - t.astype(F32)
    return terms


def _even_inproj_kernel(x_ref, g_ref, w_ref, gq_ref, gk_ref, grp_ref,
                        gla_ref, va_ref, qb_ref, kb_ref, vb_ref, *, n_sub):
    c0 = GLA_F32_W
    c1 = c0 + GLA_V
    c2 = c1 + MOBA_W
    c3 = c2 + MOBA_W
    sub_rows = x_ref.shape[0] // n_sub

    def headnorm(y, gain):
        ms = _dot((y * y).astype(BF16), grp_ref[...])
        return (y * lax.rsqrt(ms + EPS) * gain).astype(BF16)

    for s in range(n_sub):
        rows = slice(s * sub_rows, (s + 1) * sub_rows)
        h = _rmsnorm_rows(x_ref[rows, :], g_ref[...]).astype(BF16)
        qb = _dot(h, w_ref[:, c1:c2])
        kb = _dot(h, w_ref[:, c2:c3])
        gla_ref[rows, :] = _dot(h, w_ref[:, 0:c0])
        va_ref[rows, :] = _dot(h, w_ref[:, c0:c1]).astype(BF16)
        vb_ref[rows, :] = _dot(h, w_ref[:, c3:c3 + MOBA_W]).astype(BF16)
        qb_ref[rows, :] = headnorm(qb, gq_ref[...])
        kb_ref[rows, :] = headnorm(kb, gk_ref[...])


def _even_inproj(x2, g, w, gq, gk, grp, tm):
    m = x2.shape[0]
    const = lambda i: (0, 0)
    row = lambda i: (i, 0)
    return pl.pallas_call(
        functools.partial(_even_inproj_kernel, n_sub=2),
        grid=(m // tm,),
        in_specs=[
            pl.BlockSpec((tm, D_MODEL), row),
            pl.BlockSpec((1, D_MODEL), const),
            pl.BlockSpec((D_MODEL, EVEN_N), const, pipeline_mode=pl.Buffered(1)),
            pl.BlockSpec((1, MOBA_W), const),
            pl.BlockSpec((1, MOBA_W), const),
            pl.BlockSpec((MOBA_W, MOBA_W), const),
        ],
        out_specs=[
            pl.BlockSpec((tm, GLA_F32_W), row),
            pl.BlockSpec((tm, GLA_V), row),
            pl.BlockSpec((tm, MOBA_W), row),
            pl.BlockSpec((tm, MOBA_W), row),
            pl.BlockSpec((tm, MOBA_W), row),
        ],
        out_shape=[
            jax.ShapeDtypeStruct((m, GLA_F32_W), F32),
            jax.ShapeDtypeStruct((m, GLA_V), BF16),
            jax.ShapeDtypeStruct((m, MOBA_W), BF16),
            jax.ShapeDtypeStruct((m, MOBA_W), BF16),
            jax.ShapeDtypeStruct((m, MOBA_W), BF16),
        ],
        compiler_params=pltpu.CompilerParams(
            dimension_semantics=("parallel",), vmem_limit_bytes=VMEM_LIMIT),
        name="even_inproj",
    )(x2, g, w, gq, gk, grp)


def _gla_kernel(gla_ref, va_ref, wg_ref, bg_ref, gn_ref, tri_ref, o_ref, state_ref, *, sub_tile):
    n_sub = gla_ref.shape[0] // sub_tile
    n_chunks = sub_tile // GLA_CHUNK
    heads = range(GLA_HEADS)
    chunks = [slice(c * GLA_CHUNK, (c + 1) * GLA_CHUNK) for c in range(n_chunks)]
    r0 = 2 * GLA_QK
    tri = tri_ref[...]
    causal = tri > 0
    lane = lax.broadcasted_iota(jnp.int32, (sub_tile, LANES), 1)

    @pl.when(pl.program_id(1) == 0)
    def _():
        state_ref[...] = jnp.zeros_like(state_ref)

    def pair_slice(head):
        return slice((head // 2) * LANES, (head // 2 + 1) * LANES)

    def in_head(head):
        return (lane < GLA_DK) if head % 2 == 0 else (lane >= GLA_DK)

    pre = []
    for s in range(n_sub):
        rows = slice(s * sub_tile, (s + 1) * sub_tile)
        glr = gla_ref[rows, r0 + GLA_V:r0 + GLA_V + LANES].astype(BF16)
        z = _dot(glr, wg_ref[...]) + bg_ref[...]
        log_a = (jnp.minimum(z, 0.0) - jnp.log1p(jnp.exp(-jnp.abs(z)))) * (1.0 / GLA_GATE_TEMP)
        g_hi, g_lo = _split2(log_a)
        bb = _dot(tri, jnp.concatenate([g_hi, g_lo], axis=1))
        b = bb[:, 0:GLA_QK] + bb[:, GLA_QK:2 * GLA_QK]
        b_last = [b[(c + 1) * GLA_CHUNK - 1:(c + 1) * GLA_CHUNK, :] for c in range(n_chunks)]
        btot = jnp.concatenate([jnp.broadcast_to(bl, (GLA_CHUNK, GLA_QK)) for bl in b_last], axis=0)
        q_ = gla_ref[rows, 0:GLA_QK] * (GLA_DK ** -0.5) * jnp.exp(b)
        k = gla_ref[rows, GLA_QK:2 * GLA_QK]
        k_ = k * jnp.exp(-b)
        k_end = k * jnp.exp(btot - b)
        qm = [jnp.where(in_head(h), q_[:, pair_slice(h)], 0.0).astype(BF16) for h in heads]
        kem = [jnp.where(in_head(h), k_end[:, pair_slice(h)], 0.0).astype(BF16) for h in heads]
        v = [va_ref[rows, h * GLA_DV:(h + 1) * GLA_DV] for h in heads]
        d_state = [[_dot_tn(v[h][cr], kem[h][cr]) for cr in chunks] for h in heads]
        att = [jnp.where(causal, _dot_nt(qm[h], k_[:, pair_slice(h)].astype(BF16)), 0.0).astype(BF16)
               for h in heads]
        o_intra = [_dot(att[h], v[h]) for h in heads]
        decay = [[jnp.exp(bl[:, pair_slice(h)]) for bl in b_last] for h in heads]
        pre.append((qm, d_state, o_intra, decay))

    states = [[] for _ in range(n_sub)]
    for h in heads:
        st = state_ref[h]
        for s in range(n_sub):
            _, d_state, _, decay = pre[s]
            per_chunk = []
            for c in range(n_chunks):
                per_chunk.append(st.astype(BF16))
                st = st * decay[h][c] + d_state[h][c]
            states[s].append(per_chunk)
        state_ref[h] = st

    for s in range(n_sub):
        rows = slice(s * sub_tile, (s + 1) * sub_tile)
        qm, _, o_intra, _ = pre[s]
        for h in heads:
            o_inter = jnp.concatenate(
                [_dot_nt(qm[h][cr], states[s][h][c]) for c, cr in enumerate(chunks)], axis=0)
            on = _rmsnorm_rows(o_intra[h] + o_inter, gn_ref[...])
            r = gla_ref[rows, r0 + h * GLA_DV:r0 + (h + 1) * GLA_DV]
            o_ref[rows, h * GLA_DV:(h + 1) * GLA_DV] = (on * (r * jax.nn.sigmoid(r))).astype(BF16)


def _gla(gla_in, va, wg, bg, gn, tri, batch, seq, tile, sub_tile):
    nt = seq // tile
    const = lambda b, t: (0, 0)
    row = lambda b, t: (b * nt + t, 0)
    return pl.pallas_call(
        functools.partial(_gla_kernel, sub_tile=sub_tile),
        grid=(batch, nt),
        in_specs=[
            pl.BlockSpec((tile, GLA_F32_W), row),
            pl.BlockSpec((tile, GLA_V), row),
            pl.BlockSpec((LANES, GLA_QK), const),
            pl.BlockSpec((1, GLA_QK), const),
            pl.BlockSpec((1, GLA_DV), const),
            pl.BlockSpec((sub_tile, sub_tile), const),
        ],
        out_specs=pl.BlockSpec((tile, GLA_V), row),
        out_shape=jax.ShapeDtypeStruct((batch * seq, GLA_V), BF16),
        scratch_shapes=[pltpu.VMEM((GLA_HEADS, GLA_DV, LANES), F32)],
        compiler_params=pltpu.CompilerParams(
            dimension_semantics=("parallel", "arbitrary"), vmem_limit_bytes=VMEM_LIMIT),
        name="gla",
    )(gla_in, va, wg, bg, gn, tri)


def _moba_kernel(slopes_ref, q_ref, k_ref, v_ref, kaux_ref, o_ref,
                 kaug_ref, vt_ref, qt_ref, qaug_ref, km_ref, *, n_blocks):
    bs = MOBA_BLOCK
    nb = n_blocks
    n_pairs = nb // 2
    pw = 2 * bs
    bias_rows = N_TERMS * nb
    n_aux = LANES - bias_rows
    hp = pl.program_id(1)
    lane1 = lax.broadcasted_iota(jnp.int32, (1, LANES), 1)

    for jj in range(n_pairs):
        rows = slice(jj * pw, (jj + 1) * pw)
        kaug_ref[jj, :, 0:LANES] = k_ref[rows, :]
        kaug_ref[jj, :, LANES:2 * LANES] = kaux_ref[rows, :]
        vt = jnp.transpose(v_ref[rows, :])
        vrow = lax.broadcasted_iota(jnp.int32, (LANES, pw), 0)
        vt_ref[0, jj] = jnp.where(vrow < MOBA_DH, vt, jnp.ones_like(vt))
        vt_ref[1, jj] = jnp.where(vrow >= MOBA_DH, vt, jnp.ones_like(vt))
        qt_ref[jj] = jnp.transpose(q_ref[rows, :])
    for j in range(nb):
        km = jnp.sum(k_ref[j * bs:(j + 1) * bs, :].astype(F32), axis=0, keepdims=True) * (1.0 / bs)
        km_ref[j:j + 1, :] = jnp.where(lane1 < MOBA_DH, km, 0.0)
        km_ref[nb + j:nb + j + 1, :] = jnp.where(lane1 >= MOBA_DH, km, 0.0)
    km_hi, km_lo = _split2(km_ref[...])
    aux_row = lax.broadcasted_iota(jnp.int32, (n_aux, pw), 0)
    t_lo = (lax.broadcasted_iota(jnp.int32, (n_aux, pw), 1) % bs).astype(F32)
    for sub in range(2):
        slope2 = slopes_ref[2 * hp + sub] * LOG2E
        rows = _split_terms(-slope2 * t_lo, N_TERMS) + _split_terms(jnp.full((n_aux, pw), slope2, F32), N_TERMS)
        aux = jnp.zeros((n_aux, pw), F32)
        for r, term in enumerate(rows):
            aux = jnp.where(aux_row == r, term.astype(F32), aux)
        aux = aux.astype(BF16)
        for u in range(n_pairs):
            qaug_ref[sub, u, LANES + bias_rows:2 * LANES, :] = aux

    jrow = lax.broadcasted_iota(jnp.int32, (nb, pw), 0)
    later_block = lax.broadcasted_iota(jnp.int32, (nb, pw), 1) // bs
    drow = lax.broadcasted_iota(jnp.int32, (LANES, pw), 0)

    for u in range(n_pairs):
        qt = qt_ref[u]
        qblk = 2 * u + later_block
        gate_both = _dot(km_hi, qt) + _dot(km_lo, qt)
        for sub in range(2):
            slope2 = slopes_ref[2 * hp + sub] * LOG2E
            gate = jnp.where(jrow < qblk, gate_both[sub * nb:(sub + 1) * nb], -jnp.inf)
            rank = jnp.zeros((nb, pw), jnp.int32)
            for jp in range(2 * u + 1):
                g_jp = gate[jp:jp + 1, :]
                tie = jnp.where(g_jp == gate, jnp.where(jrow > jp, 1, 0), 0)
                rank = rank + jnp.where(g_jp > gate, 1, tie)
            rank = jnp.where(jrow < qblk, rank, jnp.where(jrow == qblk, 0, MOBA_TOPK))
            dist_blocks = (qblk - jrow).astype(F32) * float(bs)
            bias = jnp.where(rank < MOBA_TOPK, -slope2 * dist_blocks, NEG)
            in_head = (drow < MOBA_DH) if sub == 0 else (drow >= MOBA_DH)
            qaug_ref[sub, u, 0:LANES, :] = jnp.where(in_head, qt, jnp.zeros_like(qt))
            for r, term in enumerate(_split_terms(bias, N_TERMS)):
                qaug_ref[sub, u, LANES + r * nb:LANES + (r + 1) * nb, :] = term

    causal = (lax.broadcasted_iota(jnp.int32, (pw, pw), 0)
              <= lax.broadcasted_iota(jnp.int32, (pw, pw), 1))

    def score(u, jj, sub):
        return _dot(kaug_ref[jj], qaug_ref[sub, u])

    def update(m, acc, s, jj, sub):
        m_new = jnp.maximum(m, jnp.max(s, axis=0, keepdims=True))
        p = jnp.exp2(s - m_new).astype(BF16)
        return m_new, jnp.exp2(m - m_new) * acc + _dot(vt_ref[sub, jj], p)

    for u in range(n_pairs):
        steps = [(jj, sub) for jj in [u] + list(range(u)) for sub in range(2)]
        st = [None, None]
        keys_all = kaug_ref[0:u + 1].reshape((u + 1) * pw, 2 * LANES)
        s_all = [_dot(keys_all, qaug_ref[sub, u]) for sub in range(2)]
        for n, (jj, sub) in enumerate(steps):
            s_cur = s_all[sub][jj * pw:(jj + 1) * pw]
            if jj == u:
                s_cur = jnp.where(causal, s_cur, NEG)
                m = jnp.max(s_cur, axis=0, keepdims=True)
                st[sub] = (m, _dot(vt_ref[sub, jj], jnp.exp2(s_cur - m).astype(BF16)))
            else:
                st[sub] = update(*st[sub], s_cur, jj, sub)
        acc0, acc1 = st[0][1], st[1][1]
        out_t = jnp.where(drow < MOBA_DH, acc0 / acc0[MOBA_DH:MOBA_DH + 1], acc1 / acc1[0:1]).astype(BF16)
        o_ref[u * pw:(u + 1) * pw, :] = jnp.transpose(out_t)


def _moba(slopes, q, k, v, kaux, batch, seq):
    nb = seq // MOBA_BLOCK
    n_pairs = MOBA_HEADS // 2
    blk = lambda b, h, s: (b, h)
    return pl.pallas_call(
        functools.partial(_moba_kernel, n_blocks=nb),
        grid_spec=pltpu.PrefetchScalarGridSpec(
            num_scalar_prefetch=1,
            grid=(batch, n_pairs),
            in_specs=[
                pl.BlockSpec((seq, LANES), blk),
                pl.BlockSpec((seq, LANES), blk),
                pl.BlockSpec((seq, LANES), blk),
                pl.BlockSpec((seq, LANES), lambda b, h, s: (0, 0)),
            ],
            out_specs=pl.BlockSpec((seq, LANES), blk),
            scratch_shapes=[
                pltpu.VMEM((nb // 2, 2 * MOBA_BLOCK, 2 * LANES), BF16),
                pltpu.VMEM((2, nb // 2, LANES, 2 * MOBA_BLOCK), BF16),
                pltpu.VMEM((nb // 2, LANES, 2 * MOBA_BLOCK), BF16),
                pltpu.VMEM((2, nb // 2, 2 * LANES, 2 * MOBA_BLOCK), BF16),
                pltpu.VMEM((2 * nb, LANES), F32),
            ],
        ),
        out_shape=jax.ShapeDtypeStruct((batch * seq, MOBA_W), BF16),
        compiler_params=pltpu.CompilerParams(
            dimension_semantics=("parallel", "parallel"), vmem_limit_bytes=VMEM_LIMIT),
        name="moba",
    )(slopes, q, k, v, kaux)


def _post_kernel(*refs, n_parts, tf):
    x_ref = refs[0]
    part_refs = refs[1:1 + n_parts]
    wo_ref, g_ref, w1_ref, w2_ref, o_ref, a_ref = refs[1 + n_parts:]
    x = x_ref[...]
    row = 0
    for p_ref in part_refs:
        width = p_ref.shape[1]
        x = x + _dot(p_ref[...], wo_ref[row:row + width, :])
        row += width
    h = _rmsnorm_rows(x, g_ref[...]).astype(BF16)
    for c in range(D_FF // tf):
        cols = slice(c * tf, (c + 1) * tf)
        a = jnp.maximum(_dot(h, w1_ref[:, cols]), 0.0)
        a_ref[:, cols] = (a * a).astype(BF16)
    o_ref[...] = x + _dot(a_ref[...], w2_ref[...])


def _post(x2, parts, wo, g, w1, w2, tm, tf):
    m = x2.shape[0]
    const = lambda i: (0, 0)
    row = lambda i: (i, 0)
    resident = dict(pipeline_mode=pl.Buffered(1))
    return pl.pallas_call(
        functools.partial(_post_kernel, n_parts=len(parts), tf=tf),
        grid=(m // tm,),
        in_specs=(
            [pl.BlockSpec((tm, D_MODEL), row)]
            + [pl.BlockSpec((tm, p.shape[1]), row) for p in parts]
            + [pl.BlockSpec((D_MODEL, D_MODEL), const, **resident),
               pl.BlockSpec((1, D_MODEL), const),
               pl.BlockSpec((D_MODEL, D_FF), const, **resident),
               pl.BlockSpec((D_FF, D_MODEL), const, **resident)]),
        out_specs=pl.BlockSpec((tm, D_MODEL), row),
        out_shape=jax.ShapeDtypeStruct((m, D_MODEL), F32),
        scratch_shapes=[pltpu.VMEM((tm, D_FF), BF16)],
        compiler_params=pltpu.CompilerParams(
            dimension_semantics=("parallel",), vmem_limit_bytes=POST_VMEM_LIMIT),
        name="post",
    )(x2, *parts, wo, g, w1, w2)


def _odd_kernel(x_ref, g_ref, win_ref, cw_ref, pw_ref, ps_ref, mix_ref, zbuf_ref, ubuf_ref, *, tile):
    t_idx = pl.program_id(1)
    n_sub = 4
    sub_rows = tile // n_sub

    @pl.when(t_idx == 0)
    def _():
        zbuf_ref[0:HALO, :] = jnp.zeros((HALO, CONV_CH), F32)
        ubuf_ref[0:HALO, :] = jnp.zeros((HALO, POOL_CH), F32)

    proj = []
    for s in range(n_sub):
        rows = slice(s * sub_rows, (s + 1) * sub_rows)
        h = _rmsnorm_rows(x_ref[rows, :], g_ref[...]).astype(BF16)
        cg = _dot(h, win_ref[:, CONV_CH:2 * CONV_CH])
        xc = _dot(h, win_ref[:, 2 * CONV_CH:3 * CONV_CH])
        u = _dot(h, win_ref[:, 3 * CONV_CH:3 * CONV_CH + POOL_CH])
        bg = _dot(h, win_ref[:, 0:CONV_CH])
        proj.append((bg, cg, xc, u))

    for s in range(n_sub):
        bg, cg, xc, u = proj[s]
        rows = slice(s * sub_rows, (s + 1) * sub_rows)
        base = HALO + s * sub_rows
        z = cg * xc
        zbuf_ref[base:base + sub_rows, :] = z
        y = z * cw_ref[CONV_K - 1:CONV_K, :]
        for tap in range(CONV_K - 1):
            shift = CONV_K - 1 - tap
            y = y + zbuf_ref[base - shift:base - shift + sub_rows, :] * cw_ref[tap:tap + 1, :]
        mix_ref[rows, 0:CONV_CH] = (bg * y).astype(BF16)

        ubuf_ref[base:base + sub_rows, :] = u
        pos = (lax.broadcasted_iota(jnp.int32, (sub_rows, POOL_GROUP), 0)
               + (t_idx * tile + s * sub_rows))
        for grp, win in enumerate(POOL_WINDOWS):
            cs = slice(grp * POOL_GROUP, (grp + 1) * POOL_GROUP)
            u_g = u[:, cs]
            tot = u_g
            for shift in range(1, win):
                tot = tot + ubuf_ref[base - shift:base - shift + sub_rows, cs]
            cnt = jnp.minimum(pos + 1, win).astype(F32)
            pooled = (tot / cnt - u_g).astype(BF16)
            y_g = _dot(pooled, pw_ref[grp]) * ps_ref[:, cs]
            mix_ref[rows, CONV_CH + grp * POOL_GROUP:CONV_CH + (grp + 1) * POOL_GROUP] = y_g.astype(BF16)

    zbuf_ref[0:HALO, :] = zbuf_ref[tile:tile + HALO, :]
    ubuf_ref[0:HALO, :] = ubuf_ref[tile:tile + HALO, :]


def _odd_mix(x2, g, win, cw, pw, ps, batch, seq, tile):
    nt = seq // tile
    const = lambda b, t: (0, 0)
    row = lambda b, t: (b * nt + t, 0)
    n_in = 3 * CONV_CH + POOL_CH
    return pl.pallas_call(
        functools.partial(_odd_kernel, tile=tile),
        grid=(batch, nt),
        in_specs=[
            pl.BlockSpec((tile, D_MODEL), row),
            pl.BlockSpec((1, D_MODEL), const),
            pl.BlockSpec((D_MODEL, n_in), const),
            pl.BlockSpec((CONV_K, CONV_CH), const),
            pl.BlockSpec((len(POOL_WINDOWS), POOL_GROUP, POOL_GROUP), lambda b, t: (0, 0, 0)),
            pl.BlockSpec((1, POOL_CH), const),
        ],
        out_specs=pl.BlockSpec((tile, CONV_CH + POOL_CH), row),
        out_shape=jax.ShapeDtypeStruct((batch * seq, CONV_CH + POOL_CH), BF16),
        scratch_shapes=[
            pltpu.VMEM((tile + HALO, CONV_CH), F32),
            pltpu.VMEM((tile + HALO, POOL_CH), F32),
        ],
        compiler_params=pltpu.CompilerParams(
            dimension_semantics=("parallel", "arbitrary"), vmem_limit_bytes=VMEM_LIMIT),
        name="odd_mix",
    )(x2, g, win, cw, pw, ps)


def _gla_tri(tile):
    r = np.arange(tile)
    same = (r[:, None] // GLA_CHUNK) == (r[None, :] // GLA_CHUNK)
    lower = same & (r[None, :] <= r[:, None])
    return jnp.asarray(lower.astype(np.float32), dtype=BF16)


def _moba_kaux(seq):
    pos = np.arange(seq)
    nb = seq // MOBA_BLOCK
    aux = np.zeros((seq, LANES), np.float32)
    for r in range(N_TERMS):
        aux[pos, r * nb + pos // MOBA_BLOCK] = 1.0
        aux[:, N_TERMS * nb + r] = 1.0
        aux[:, N_TERMS * (nb + 1) + r] = pos % MOBA_BLOCK
    return jnp.asarray(aux, dtype=BF16)


def _head_group_matrix():
    r = np.arange(MOBA_W)
    same = (r[:, None] // MOBA_DH) == (r[None, :] // MOBA_DH)
    return jnp.asarray(same.astype(np.float32) / MOBA_DH, dtype=BF16)


def _even_weight(w_in):
    qa, ka, va, ra, glr, qb, kb, vb = jnp.split(
        w_in, np.cumsum([GLA_QK, GLA_QK, GLA_V, GLA_V, GLA_GATE_RANK, MOBA_W, MOBA_W])[:].tolist(), axis=-1)
    glr_pad = jnp.pad(glr, ((0, 0), (0, LANES - GLA_GATE_RANK)))
    return jnp.concatenate([qa, ka, ra, glr_pad, va, qb, kb, vb], axis=-1).astype(BF16)


def kernel(x, norm_mix, norm_mlp, w_o, w1, w2, even_w_in, gla_w_gate2, gla_b_gate, gla_out_norm,
           moba_q_norm, moba_k_norm, odd_w_in, conv_w, pool_w, pool_scale):
    batch, seq, d = x.shape
    assert d == D_MODEL and seq % (4 * MOBA_BLOCK) == 0
    assert N_TERMS * (seq // MOBA_BLOCK + 2) <= LANES and seq // MOBA_BLOCK % 16 == 0
    depth = norm_mix.shape[0]
    m = batch * seq
    tm = 1024
    gla_tile = 1024
    gla_sub = 256
    odd_tile = 1024
    post_tm = 1024
    post_tf = 512

    tri = _gla_tri(gla_sub)
    kaux = _moba_kaux(seq)
    grp = _head_group_matrix()
    slopes = jnp.exp2(-8.0 * jnp.arange(1, MOBA_HEADS + 1, dtype=F32) / MOBA_HEADS)

    x2 = x.reshape(m, d)
    for i in range(depth):
        j = i // 2
        g_mix = norm_mix[i].reshape(1, d)
        wo = w_o[i].astype(BF16)
        if i % 2 == 0:
            w_in = _even_weight(even_w_in[j])
            gq = (jnp.tile(moba_q_norm[j], MOBA_HEADS) * (MOBA_DH ** -0.5 * LOG2E)).reshape(1, MOBA_W)
            gk = jnp.tile(moba_k_norm[j], MOBA_HEADS).reshape(1, MOBA_W)
            gla_in, va, qb, kb, vb = _even_inproj(x2, g_mix, w_in, gq, gk, grp, tm)
            wg = jnp.pad(gla_w_gate2[j], ((0, LANES - GLA_GATE_RANK), (0, 0))).astype(BF16)
            oa = _gla(gla_in, va, wg, gla_b_gate[j].reshape(1, GLA_QK),
                      gla_out_norm[j].reshape(1, GLA_DV), tri, batch, seq, gla_tile, gla_sub)
            ob = _moba(slopes, qb, kb, vb, kaux, batch, seq)
            parts = [oa, ob]
        else:
            parts = [_odd_mix(x2, g_mix, odd_w_in[j].astype(BF16), conv_w[j], pool_w[j].astype(BF16),
                              pool_scale[j].reshape(1, POOL_CH), batch, seq, odd_tile)]
        x2 = _post(x2, parts, wo, norm_mlp[i].reshape(1, d), w1[i].astype(BF16), w2[i].astype(BF16),
                   post_tm, post_tf)
    return x2.reshape(batch, seq, d)
```
